```python
import math
import jax
import jax.numpy as jnp
from jax import lax
import numpy as np

D_MODEL = 1024
BATCH = 8
SEQ = 4096
DEPTH = 4

N_EVEN = (DEPTH + 1) // 2
N_ODD = DEPTH // 2
MIX_WIDTH = D_MODEL
NORM_EPS = 1e-6

S5_GROUP = 16
S5_STATE = 64
S5_WIDTH = MIX_WIDTH // 2
S5_GROUPS = S5_WIDTH // S5_GROUP

RWKV_HEAD = 64
RWKV_WIDTH = MIX_WIDTH - S5_WIDTH
RWKV_HEADS = RWKV_WIDTH // RWKV_HEAD
DECAY_LORA = 64
ICLR_LORA = 64
GATE_LORA = 128
RWKV_SHIFTED = 3 * RWKV_WIDTH + DECAY_LORA + ICLR_LORA + GATE_LORA
RWKV_DECAY_SCALE = math.exp(-0.5)
RWKV_LN_EPS = 1e-5 * RWKV_HEAD
EVEN_IN = S5_WIDTH + RWKV_SHIFTED + MIX_WIDTH

GDN_HEAD = 128
GDN_HEADS = MIX_WIDTH // GDN_HEAD
GDN_CONV = 4
GDN_CHUNK = 64
ODD_IN = 3 * MIX_WIDTH + 2 * GDN_HEADS + MIX_WIDTH

kernel_name = 'hybrid_s5_rwkv7_gdn_adaln'


def rms_norm(x, w, eps=NORM_EPS):
    xf = x.astype(jnp.float32)
    y = xf * lax.rsqrt(jnp.mean(xf * xf, axis=-1, keepdims=True) + eps)
    return y.astype(x.dtype) * w


def l2_normalize(t, eps=1e-6):
    return t * lax.rsqrt(jnp.sum(t * t, axis=-1, keepdims=True) + eps)


def token_shift(t):
    return jnp.pad(t, ((0, 0), (1, 0), (0, 0)))[:, :-1]


def causal_depthwise_conv(x, w):
    k_len, ch = w.shape
    xp = jnp.pad(x, ((0, 0), (k_len - 1, 0), (0, 0)))
    return lax.conv_general_dilated(xp, w[:, None, :], window_strides=(1,), padding='VALID',
                                    dimension_numbers=('NWC', 'WIO', 'NWC'), feature_group_count=ch)


def _diag_linear_combine(e1, e2):
    a1, b1 = e1
    a2, b2 = e2
    return a1 * a2, a2 * b1 + b2


def s5_mixer(u, lambda_re, lambda_im, log_step, b_re, b_im, c_re, c_im, d_skip, glu_w, glu_b):
    bsz, seq, _ = u.shape
    f32 = jnp.float32
    uf = u.astype(f32)
    lam = lax.complex(jnp.minimum(lambda_re.astype(f32), -1e-4), lambda_im.astype(f32))
    step = jnp.exp(log_step.astype(f32))[:, None]
    lam_bar = jnp.exp(lam * step)
    b_bar = ((lam_bar - 1.0) / lam)[..., None] * lax.complex(b_re.astype(f32), b_im.astype(f32))
    c_mat = lax.complex(c_re.astype(f32), c_im.astype(f32))
    u_grp = uf.reshape(bsz, seq, S5_GROUPS, S5_GROUP).astype(jnp.complex64)
    bu = jnp.einsum('gpc,blgc->lbgp', b_bar, u_grp)
    a = jnp.broadcast_to(lam_bar[None, None], (seq, 1, S5_GROUPS, S5_STATE))
    _, states = lax.associative_scan(_diag_linear_combine, (a, bu), axis=0)
    y = jnp.real(jnp.einsum('gcp,lbgp->blgc', c_mat, states)).reshape(bsz, seq, S5_WIDTH)
    y = jax.nn.gelu(y + d_skip * uf)
    return y * jax.nn.sigmoid(y @ glu_w + glu_b)


def rwkv7_mixer(feats, mu, w0, w_up, a0, a_up, g_up, k_k, k_a, r_k, ln_w, ln_b):
    f = feats.astype(jnp.float32)
    f = f + mu * (token_shift(f) - f)
    w1 = RWKV_WIDTH
    r, k, v, xw, xa, xg = jnp.split(f, [w1, 2 * w1, 3 * w1, 3 * w1 + DECAY_LORA, 3 * w1 + DECAY_LORA + ICLR_LORA], axis=-1)
    log_decay = -RWKV_DECAY_SCALE * jax.nn.sigmoid(w0 + jnp.tanh(xw) @ w_up)
    a = jax.nn.sigmoid(a0 + xa @ a_up)
    g = jax.nn.sigmoid(xg) @ g_up
    bsz, seq, _ = f.shape
    heads = lambda t: t.reshape(bsz, seq, RWKV_HEADS, RWKV_HEAD)
    r, k, v, a, decay = heads(r), heads(k), heads(v), heads(a), heads(jnp.exp(log_decay))
    kk = l2_normalize(k * k_k.reshape(RWKV_HEADS, RWKV_HEAD))
    k = k * (1.0 + (a - 1.0) * k_a.reshape(RWKV_HEADS, RWKV_HEAD))

    def step(state, inp):
        r_t, w_t, k_t, v_t, kk_t, a_t = inp
        removed = jnp.einsum('bhvk,bhk->bhv', state, kk_t)
        state = (state * w_t[:, :, None, :]
                 - removed[..., None] * (kk_t * a_t)[:, :, None, :]
                 + v_t[..., None] * k_t[:, :, None, :])
        return state, jnp.einsum('bhvk,bhk->bhv', state, r_t)

    time_major = lambda t: jnp.moveaxis(t, 1, 0)
    state0 = jnp.zeros((bsz, RWKV_HEADS, RWKV_HEAD, RWKV_HEAD), jnp.float32)
    _, o = lax.scan(step, state0, (time_major(r), time_major(decay), time_major(k),
                                   time_major(v), time_major(kk), time_major(a)))
    o = jnp.moveaxis(o, 0, 1)
    mean = jnp.mean(o, axis=-1, keepdims=True)
    var = jnp.mean(jnp.square(o - mean), axis=-1, keepdims=True)
    o = ((o - mean) * lax.rsqrt(var + RWKV_LN_EPS)).reshape(bsz, seq, RWKV_WIDTH) * ln_w + ln_b
    bonus = jnp.sum(r * k * r_k.reshape(RWKV_HEADS, RWKV_HEAD), axis=-1, keepdims=True) * v
    return (o + bonus.reshape(bsz, seq, RWKV_WIDTH)) * g


def chunk_gated_delta_rule(q, k, v, log_decay, beta):
    bsz, seq, nh, dk = q.shape
    dv = v.shape[-1]
    n_chunks = seq // GDN_CHUNK
    cs = GDN_CHUNK
    chunks = lambda t: jnp.moveaxis(t.reshape(bsz, n_chunks, cs, nh, -1), 3, 1)
    q, k, v = chunks(q), chunks(k), chunks(v)
    beta = chunks(beta[..., None])[..., 0]
    g = jnp.cumsum(chunks(log_decay[..., None])[..., 0], axis=-1)
    causal = jnp.tril(jnp.ones((cs, cs), bool))
    strict = jnp.tril(jnp.ones((cs, cs), bool), -1)
    decay = jnp.exp(jnp.where(causal, g[..., :, None] - g[..., None, :], -jnp.inf))
    k_beta = k * beta[..., None]
    a_mat = jnp.where(strict, jnp.einsum('bhncd,bhnsd->bhncs', k_beta, k) * decay, 0.0)
    eye = jnp.eye(cs, dtype=q.dtype)
    t_mat = lax.linalg.triangular_solve(eye + a_mat, jnp.broadcast_to(eye, a_mat.shape),
                                        left_side=True, lower=True, unit_diagonal=True)
    u = t_mat @ (v * beta[..., None])
    w = t_mat @ (k_beta * jnp.exp(g)[..., None])
    intra = jnp.where(causal, jnp.einsum('bhncd,bhnsd->bhncs', q, k) * decay, 0.0)
    q_decayed = q * jnp.exp(g)[..., None]
    k_to_end = k * jnp.exp(g[..., -1:] - g)[..., None]
    chunk_decay = jnp.exp(g[..., -1])

    def step(state, inp):
        q_i, k_i, u_i, w_i, intra_i, cd_i = inp
        v_new = u_i - jnp.einsum('bhcd,bhdv->bhcv', w_i, state)
        o_i = jnp.einsum('bhcd,bhdv->bhcv', q_i, state) + jnp.einsum('bhcs,bhsv->bhcv', intra_i, v_new)
        state = state * cd_i[..., None, None] + jnp.einsum('bhcd,bhcv->bhdv', k_i, v_new)
        return state, o_i

    by_chunk = lambda t: jnp.moveaxis(t, 2, 0)
    state0 = jnp.zeros((bsz, nh, dk, dv), q.dtype)
    _, o = lax.scan(step, state0, (by_chunk(q_decayed), by_chunk(k_to_end), by_chunk(u),
                                   by_chunk(w), by_chunk(intra), by_chunk(chunk_decay)))
    return jnp.transpose(o, (1, 0, 3, 2, 4)).reshape(bsz, seq, nh, dv)


def gated_deltanet_mixer(qkv, beta_raw, alpha_raw, conv_w, a_log, dt_bias, norm_w):
    bsz, seq, _ = qkv.shape
    f32 = jnp.float32
    qkv = jax.nn.silu(causal_depthwise_conv(qkv.astype(f32), conv_w.astype(f32)))
    q, k, v = jnp.split(qkv, 3, axis=-1)
    heads = lambda t: t.reshape(bsz, seq, GDN_HEADS, GDN_HEAD)
    q = l2_normalize(heads(q)) * (GDN_HEAD ** -0.5)
    k = l2_normalize(heads(k))
    v = heads(v)
    beta = jax.nn.sigmoid(beta_raw.astype(f32))
    log_decay = -jnp.exp(a_log.astype(f32)) * jax.nn.softplus(alpha_raw.astype(f32) + dt_bias)
    o = chunk_gated_delta_rule(q, k, v, log_decay, beta)
    o = o * lax.rsqrt(jnp.mean(o * o, axis=-1, keepdims=True) + NORM_EPS) * norm_w
    return o.reshape(bsz, seq, MIX_WIDTH)


def setup_inputs(seed: int = 0) -> dict:
    key = jax.random.key(seed)
    ks = list(jax.random.split(key, 40))
    nrm = lambda shape, std: std * jax.random.normal(ks.pop(), shape, jnp.float32)
    uni = lambda shape, lo, hi: jax.random.uniform(ks.pop(), shape, jnp.float32, lo, hi)
    n_idx = jnp.arange(S5_STATE, dtype=jnp.float32)
    dt = jnp.exp(uni((N_ODD, GDN_HEADS), math.log(1e-3), math.log(1e-1)))
    return {
        'x': nrm((BATCH, SEQ, D_MODEL), 1.0),
        'c': nrm((BATCH, D_MODEL), 1.0),
        'norm_w': 1.0 + nrm((DEPTH, D_MODEL), 0.02),
        'ada_w': nrm((DEPTH, D_MODEL, 3 * D_MODEL), 0.5 * D_MODEL ** -0.5),
        'ada_b': nrm((DEPTH, 3 * D_MODEL), 0.02),
        'w_out': nrm((DEPTH, MIX_WIDTH, D_MODEL), MIX_WIDTH ** -0.5),
        'final_norm_w': 1.0 + nrm((D_MODEL,), 0.02),
        'even_w_in': nrm((N_EVEN, D_MODEL, EVEN_IN), D_MODEL ** -0.5),
        's5_lambda_re': -0.5 + nrm((N_EVEN, S5_GROUPS, S5_STATE), 0.01),
        's5_lambda_im': math.pi * n_idx + nrm((N_EVEN, S5_GROUPS, S5_STATE), 0.01),
        's5_log_step': uni((N_EVEN, S5_GROUPS), math.log(1e-3), math.log(1e-1)),
        's5_b_re': nrm((N_EVEN, S5_GROUPS, S5_STATE, S5_GROUP), (2 * S5_GROUP) ** -0.5),
        's5_b_im': nrm((N_EVEN, S5_GROUPS, S5_STATE, S5_GROUP), (2 * S5_GROUP) ** -0.5),
        's5_c_re': nrm((N_EVEN, S5_GROUPS, S5_GROUP, S5_STATE), 0.5),
        's5_c_im': nrm((N_EVEN, S5_GROUPS, S5_GROUP, S5_STATE), 0.5),
        's5_d': nrm((N_EVEN, S5_WIDTH), 0.5),
        's5_glu_w': nrm((N_EVEN, S5_WIDTH, S5_WIDTH), S5_WIDTH ** -0.5),
        's5_glu_b': nrm((N_EVEN, S5_WIDTH), 0.02),
        'rwkv_mu': uni((N_EVEN, RWKV_SHIFTED), 0.0, 1.0),
        'rwkv_w0': uni((N_EVEN, RWKV_WIDTH), -4.0, 2.0),
        'rwkv_w_up': nrm((N_EVEN, DECAY_LORA, RWKV_WIDTH), 0.1),
        'rwkv_a0': nrm((N_EVEN, RWKV_WIDTH), 0.1),
        'rwkv_a_up': nrm((N_EVEN, ICLR_LORA, RWKV_WIDTH), 0.1),
        'rwkv_g_up': nrm((N_EVEN, GATE_LORA, RWKV_WIDTH), GATE_LORA ** -0.5),
        'rwkv_k_k': 0.85 + nrm((N_EVEN, RWKV_WIDTH), 0.02),
        'rwkv_k_a': 1.0 + nrm((N_EVEN, RWKV_WIDTH), 0.02),
        'rwkv_r_k': nrm((N_EVEN, RWKV_WIDTH), 0.1),
        'rwkv_ln_w': 1.0 + nrm((N_EVEN, RWKV_WIDTH), 0.02),
        'rwkv_ln_b': nrm((N_EVEN, RWKV_WIDTH), 0.02),
        'odd_w_in': nrm((N_ODD, D_MODEL, ODD_IN), D_MODEL ** -0.5),
        'gdn_conv_w': nrm((N_ODD, GDN_CONV, 3 * MIX_WIDTH), GDN_CONV ** -0.5),
        'gdn_a_log': jnp.log(uni((N_ODD, GDN_HEADS), 1.0, 16.0)),
        'gdn_dt_bias': dt + jnp.log(-jnp.expm1(-dt)),
        'gdn_norm_w': 1.0 + nrm((N_ODD, GDN_HEAD), 0.02),
    }


def reference(x, c, norm_w, ada_w, ada_b, w_out, final_norm_w, even_w_in,
              s5_lambda_re, s5_lambda_im, s5_log_step, s5_b_re, s5_b_im, s5_c_re, s5_c_im,
              s5_d, s5_glu_w, s5_glu_b, rwkv_mu, rwkv_w0, rwkv_w_up, rwkv_a0, rwkv_a_up,
              rwkv_g_up, rwkv_k_k, rwkv_k_a, rwkv_r_k, rwkv_ln_w, rwkv_ln_b,
              odd_w_in, gdn_conv_w, gdn_a_log, gdn_dt_bias, gdn_norm_w):
    mod = jnp.einsum('bd,lde->lbe', jax.nn.silu(c), ada_w) + ada_b[:, None, :]
    for layer in range(DEPTH):
        shift, scale, gate = jnp.split(mod[layer], 3, axis=-1)
        h = rms_norm(x, norm_w[layer]) * (1.0 + scale[:, None, :]) + shift[:, None, :]
        i = layer // 2
        if layer % 2 == 0:
            proj = h @ even_w_in[i]
            u = proj[..., :S5_WIDTH]
            feats = proj[..., S5_WIDTH:S5_WIDTH + RWKV_SHIFTED]
            z = proj[..., S5_WIDTH + RWKV_SHIFTED:]
            y_a = s5_mixer(u, s5_lambda_re[i], s5_lambda_im[i], s5_log_step[i], s5_b_re[i], s5_b_im[i],
                           s5_c_re[i], s5_c_im[i], s5_d[i], s5_glu_w[i], s5_glu_b[i])
            y_b = rwkv7_mixer(feats, rwkv_mu[i], rwkv_w0[i], rwkv_w_up[i], rwkv_a0[i], rwkv_a_up[i],
                              rwkv_g_up[i], rwkv_k_k[i], rwkv_k_a[i], rwkv_r_k[i], rwkv_ln_w[i], rwkv_ln_b[i])
            y = jnp.concatenate([y_a, y_b], axis=-1)
        else:
            proj = h @ odd_w_in[i]
            qkv = proj[..., :3 * MIX_WIDTH]
            beta_raw = proj[..., 3 * MIX_WIDTH:3 * MIX_WIDTH + GDN_HEADS]
            alpha_raw = proj[..., 3 * MIX_WIDTH + GDN_HEADS:3 * MIX_WIDTH + 2 * GDN_HEADS]
            z = proj[..., 3 * MIX_WIDTH + 2 * GDN_HEADS:]
            y = gated_deltanet_mixer(qkv, beta_raw, alpha_raw, gdn_conv_w[i], gdn_a_log[i],
                                     gdn_dt_bias[i], gdn_norm_w[i])
        y = (y * jax.nn.silu(z.astype(jnp.float32))).astype(x.dtype)
        x = x + gate[:, None, :] * (y @ w_out[layer])
    return rms_norm(x, final_norm_w)
```

```python
import functools
import math

import jax
import jax.numpy as jnp
from jax import lax
from jax.experimental import pallas as pl
from jax.experimental.pallas import tpu as pltpu

F32 = jnp.float32
BF16 = jnp.bfloat16

NORM_EPS = 1e-6
LANES = 128
SUBLANES = 8
VMEM_LIMIT = 56 * 1024 * 1024

S5_GROUP = 16
S5_STATE = 64
RWKV_HEAD = 64
DECAY_LORA = 64
ICLR_LORA = 64
GATE_LORA = 128
RWKV_DECAY_SCALE = math.exp(-0.5)
GDN_HEAD = 128
GDN_CONV = 4
CHUNK = 64
S5_T = 8

NN = (((1,), (0,)), ((), ()))
NT = (((1,), (1,)), ((), ()))


def _split(a, n):
    if a.dtype == BF16:
        return [a]
    parts = []
    r = a
    for i in range(n):
        p = r.astype(BF16)
        parts.append(p)
        if i + 1 < n:
            r = r - p.astype(F32)
    return parts


def _mm(a, b, dn=NN, pa=1, pb=1):
    ap = _split(a, pa)
    bp = _split(b, pb)
    lim = max(len(ap), len(bp))
    acc = None
    for i, x in enumerate(ap):
        for j, y in enumerate(bp):
            if i + j >= lim:
                continue
            t = lax.dot_general(x, y, dn, preferred_element_type=F32)
            acc = t if acc is None else acc + t
    return acc


def _sigmoid(x):
    return 1.0 / (1.0 + jnp.exp(-x))


def _silu(x):
    return x * _sigmoid(x)


def _softplus(x):
    return jnp.maximum(x, 0.0) + jnp.log(1.0 + jnp.exp(-jnp.abs(x)))


def _gelu_tanh(x):
    c = math.sqrt(2.0 / math.pi)
    return 0.5 * x * (1.0 + jnp.tanh(c * (x + 0.044715 * (x * x * x))))


def _iota2(shape, dim):
    return lax.broadcasted_iota(jnp.int32, shape, dim)


def _inv_unit_lower(n_mat, eye, passes):
    t = eye + n_mat
    p = n_mat
    steps = int(math.log2(CHUNK)) - 1
    for _ in range(steps):
        p = _mm(p, p, pa=passes, pb=passes)
        t = t + _mm(t, p, pa=passes, pb=passes)
    return t


def _cparams(sem):
    return pltpu.CompilerParams(dimension_semantics=sem, vmem_limit_bytes=VMEM_LIMIT)


def _mod_kernel(c_ref, w_ref, b_ref, o_ref):
    s = _silu(c_ref[...])
    o_ref[0] = _mm(s, w_ref[0], pa=2, pb=2) + b_ref[0]


def _modulation(c, ada_w, ada_b):
    depth, d, d3 = ada_w.shape
    bsz = c.shape[0]
    nj = d3 // d
    return pl.pallas_call(
        _mod_kernel,
        grid=(depth, nj),
        in_specs=[
            pl.BlockSpec((bsz, d), lambda l, j: (0, 0)),
            pl.BlockSpec((1, d, d), lambda l, j: (l, 0, j)),
            pl.BlockSpec((1, 1, d), lambda l, j: (l, 0, j)),
        ],
        out_specs=pl.BlockSpec((1, bsz, d), lambda l, j: (l, 0, j)),
        out_shape=jax.ShapeDtypeStruct((depth, bsz, d3), F32),
        compiler_params=_cparams(("arbitrary", "arbitrary")),
        name="adaln_mod",
    )(c, ada_w, ada_b.reshape(depth, 1, d3))


def _norm_mod(x, nw, scale, shift):
    ms = jnp.mean(x * x, axis=-1, keepdims=True)
    h = x * lax.rsqrt(ms + NORM_EPS) * nw
    return h * (1.0 + scale) + shift


def _inproj_even_kernel(x_ref, nw_ref, sh_ref, sc_ref, w_ref, u_ref, f_ref, z_ref, *, tn):
    hb = _norm_mod(x_ref[...], nw_ref[...], sc_ref[0], sh_ref[0]).astype(BF16)
    nu = u_ref.shape[0]
    for j in range(nu):
        u_ref[j] = _mm(hb, w_ref[:, j * LANES:(j + 1) * LANES])
    off = nu * LANES
    for o_ref in (f_ref, z_ref):
        width = o_ref.shape[1]
        for j0 in range(0, width, tn):
            w = min(tn, width - j0)
            o_ref[:, j0:j0 + w] = _mm(hb, w_ref[:, off + j0:off + j0 + w])
        off += width


def _inproj_odd_kernel(x_ref, nw_ref, sh_ref, sc_ref, w_ref, qkv_ref, z_ref, ba_ref, *, tn):
    hb = _norm_mod(x_ref[...], nw_ref[...], sc_ref[0], sh_ref[0]).astype(BF16)
    off = 0
    for o_ref in (qkv_ref, z_ref, ba_ref):
        width = o_ref.shape[1]
        for j0 in range(0, width, tn):
            w = min(tn, width - j0)
            o_ref[:, j0:j0 + w] = _mm(hb, w_ref[:, off + j0:off + j0 + w])
        off += width


def _inproj(x2, nw, shift, scale, w_bf, kernel_fn, out_shapes, out_specs, seq, tm):
    bl, d = x2.shape
    n = w_bf.shape[1]
    per_b = seq // tm
    return pl.pallas_call(
        kernel_fn,
        grid=(bl // tm,),
        in_specs=[
            pl.BlockSpec((tm, d), lambda i: (i, 0)),
            pl.BlockSpec((1, d), lambda i: (0, 0)),
            pl.BlockSpec((1, 1, d), lambda i: (i // per_b, 0, 0)),
            pl.BlockSpec((1, 1, d), lambda i: (i // per_b, 0, 0)),
            pl.BlockSpec((d, n), lambda i: (0, 0)),
        ],
        out_specs=out_specs,
        out_shape=out_shapes,
        compiler_params=_cparams(("arbitrary",)),
        name="in_proj",
    )(x2, nw, shift, scale, w_bf)


def _s5_kernel(u_ref, m_ref, p_ref, q_ref, are_ref, aim_ref, y_ref, st_ref, sloc_ref, sprev_ref, *, tc, nb):
    cb = pl.program_id(1)

    @pl.when(cb == 0)
    def _():
        st_ref[...] = jnp.zeros_like(st_ref)

    kcols = u_ref.shape[-1]
    x = u_ref[0].reshape(nb * tc, kcols).astype(BF16)
    y = _mm(x, m_ref[0])
    sloc = _mm(x, p_ref[0])
    nt2 = sloc_ref.shape[0]
    nt = nt2 // 2
    for k in range(nt2):
        sloc_ref[k] = sloc[:, k * LANES:(k + 1) * LANES]
    ns = are_ref.shape[-1]
    a_re = are_ref[0]
    a_im = aim_ref[0]

    def body(c, carry):
        rows = pl.ds(c, nb, stride=tc)
        new = []
        for k in range(nt):
            s_re, s_im = carry[k], carry[nt + k]
            sprev_ref[k, rows, :] = s_re
            sprev_ref[nt + k, rows, :] = s_im
            ar = a_re[:, k * LANES:(k + 1) * LANES]
            ai = a_im[:, k * LANES:(k + 1) * LANES]
            new.append((ar * s_re - ai * s_im + sloc_ref[k, rows, :],
                        ar * s_im + ai * s_re + sloc_ref[nt + k, rows, :]))
        return tuple(n[0] for n in new) + tuple(n[1] for n in new)

    st = lax.fori_loop(0, tc, body, tuple(st_ref[k] for k in range(nt2)))
    for k in range(nt2):
        st_ref[k] = st[k]
    sprev = jnp.concatenate([sprev_ref[k] for k in range(nt2)], axis=1)
    y = y + _mm(sprev, q_ref[0])
    y_ref[0] = y.reshape(nb, tc, kcols)


def _s5_weights(lambda_re, lambda_im, log_step, b_re, b_im, c_re, c_im):
    g_all, p_all = lambda_re.shape
    t_len = S5_T
    gpt = LANES // S5_GROUP
    nj = g_all // gpt
    lam = lax.complex(jnp.minimum(lambda_re, -1e-4), lambda_im)
    step = jnp.exp(log_step)[:, None]
    lam_bar = jnp.exp(lam * step)
    b_bar = ((lam_bar - 1.0) / lam)[..., None] * lax.complex(b_re, b_im)
    c_mat = lax.complex(c_re, c_im)
    taus = jnp.arange(t_len + 1, dtype=F32)
    pw = jnp.exp((lam * step)[None] * taus[:, None, None])
    hi = lax.Precision.HIGHEST
    kern = jnp.real(jnp.einsum('gcp,tgp,gpd->tgcd', c_mat, pw[:t_len], b_bar, precision=hi))
    s_idx = jnp.arange(t_len)
    lag = s_idx[None, :] - s_idx[:, None]
    kt = kern[jnp.clip(lag, 0, t_len - 1)]
    kt = jnp.where((lag >= 0)[:, :, None, None, None], kt, 0.0)
    kt = kt.reshape(t_len, t_len, nj, gpt, S5_GROUP, S5_GROUP)
    eye_g = jnp.eye(gpt, dtype=F32)
    m_mat = jnp.einsum('stjgoi,hg->jshitgo', kt, eye_g).reshape(nj, t_len * LANES, t_len * LANES)
    pb = pw[t_len - 1 - s_idx][:, :, :, None] * b_bar[None]
    pb = jnp.stack([jnp.real(pb), jnp.imag(pb)], axis=0)
    pb = pb.reshape(2, t_len, nj, gpt, p_all, S5_GROUP)
    p_mat = jnp.einsum('rsjgpi,hg->jshirgp', pb, eye_g).reshape(nj, t_len * LANES, 2 * gpt * p_all)
    cq = c_mat[None] * jnp.swapaxes(pw[1:t_len + 1], 1, 1)[:, :, None, :]
    cq = jnp.stack([jnp.real(cq), -jnp.imag(cq)], axis=0)
    cq = cq.reshape(2, t_len, nj, gpt, S5_GROUP, p_all)
    q_mat = jnp.einsum('rtjgop,hg->jrhptgo', cq, eye_g).reshape(nj, 2 * gpt * p_all, t_len * LANES)
    a_t = pw[t_len].reshape(nj, 1, gpt * p_all)
    return (m_mat.astype(BF16), p_mat.astype(BF16), q_mat.astype(BF16),
            jnp.real(a_t).astype(F32), jnp.imag(a_t).astype(F32))


def _s5_scan(u4, weights, bsz, seq):
    m_mat, p_mat, q_mat, a_re, a_im = weights
    nj = u4.shape[0]
    kcols = S5_T * LANES
    nchunks = seq // S5_T
    tc = min(64, nchunks)
    ns2 = p_mat.shape[-1]
    u5 = u4.reshape(nj, bsz, nchunks, kcols)
    kern = functools.partial(_s5_kernel, tc=tc, nb=bsz)
    y5 = pl.pallas_call(
        kern,
        grid=(nj, nchunks // tc),
        in_specs=[
            pl.BlockSpec((1, bsz, tc, kcols), lambda j, c: (j, 0, c, 0)),
            pl.BlockSpec((1, kcols, kcols), lambda j, c: (j, 0, 0)),
            pl.BlockSpec((1, kcols, ns2), lambda j, c: (j, 0, 0)),
            pl.BlockSpec((1, ns2, kcols), lambda j, c: (j, 0, 0)),
            pl.BlockSpec((1, 1, ns2 // 2), lambda j, c: (j, 0, 0)),
            pl.BlockSpec((1, 1, ns2 // 2), lambda j, c: (j, 0, 0)),
        ],
        out_specs=pl.BlockSpec((1, bsz, tc, kcols), lambda j, c: (j, 0, c, 0)),
        out_shape=jax.ShapeDtypeStruct((nj, bsz, nchunks, kcols), F32),
        scratch_shapes=[
            pltpu.VMEM((ns2 // LANES, bsz, LANES), F32),
            pltpu.VMEM((ns2 // LANES, bsz * tc, LANES), F32),
            pltpu.VMEM((ns2 // LANES, bsz * tc, LANES), F32),
        ],
        compiler_params=_cparams(("arbitrary", "arbitrary")),
        name="s5_scan",
    )(u5, m_mat, p_mat, q_mat, a_re, a_im)
    return y5.reshape(nj, bsz * seq, LANES)


def _rwkv_kernel(f_ref, mu_ref, w0_ref, wup_ref, a0_ref, aup_ref, gup_ref, kk_ref, ka_ref, rk_ref,
                 lnw_ref, lnb_ref, ones_ref, o_ref,
                 fbuf, state, r_s, lw_s, kn_s, b_s, kp_s, v_s, o_s, *, lb, inv_passes):
    t = pl.program_id(1)
    width = r_s.shape[1]
    npair = width // LANES

    @pl.when(t == 0)
    def _():
        fbuf[0:SUBLANES, :] = jnp.zeros((SUBLANES, fbuf.shape[1]), F32)
        state[...] = jnp.zeros_like(state)

    @pl.when(t > 0)
    def _():
        fbuf[0:SUBLANES, :] = fbuf[lb:lb + SUBLANES, :]

    x = f_ref[...]
    fbuf[SUBLANES:SUBLANES + lb, :] = x
    xs = fbuf[pl.ds(SUBLANES - 1, lb), :]
    f = x + mu_ref[...] * (xs - x)
    r = f[:, 0:width]
    k = f[:, width:2 * width]
    v = f[:, 2 * width:3 * width]
    lora = f[:, 3 * width:3 * width + LANES]
    xg = f[:, 3 * width + LANES:3 * width + 2 * LANES]
    lw = -RWKV_DECAY_SCALE * _sigmoid(w0_ref[...] + _mm(jnp.tanh(lora), wup_ref[...]))
    a = _sigmoid(a0_ref[...] + _mm(lora, aup_ref[...]))
    g = _mm(_sigmoid(xg), gup_ref[...])
    ones_bd = ones_ref[...]
    kk = k * kk_ref[...]
    kn = kk * lax.rsqrt(_mm(kk * kk, ones_bd, pa=2) + 1e-6)
    kp = k * (1.0 + (a - 1.0) * ka_ref[...])
    bonus = _mm(r * kp * rk_ref[...], ones_bd, pa=2) * v
    r_s[...] = r
    lw_s[...] = lw
    kn_s[...] = kn
    b_s[...] = kn * a
    kp_s[...] = kp
    v_s[...] = v

    c2 = 2 * CHUNK
    row = _iota2((c2, c2), 0)
    col = _iota2((c2, c2), 1)
    same = (row // CHUNK) == (col // CHUNK)
    strict = same & (row > col)
    incl = same & (row >= col)
    eye = jnp.where(row == col, 1.0, 0.0).astype(F32)
    tri = jnp.where(_iota2((CHUNK, CHUNK), 0) >= _iota2((CHUNK, CHUNK), 1), 1.0, 0.0).astype(BF16)
    first = _iota2((CHUNK, LANES), 1) < RWKV_HEAD

    def stack(m):
        return jnp.concatenate([jnp.where(first, m, 0.0), jnp.where(first, 0.0, m)], axis=0)

    def chunk_body(c, carry):
        r0 = pl.multiple_of(c * CHUNK, CHUNK)
        rows = pl.ds(r0, CHUNK)
        lwc = lw_s[rows, :]
        gc = _mm(tri, lwc, pb=3)
        e_g = jnp.exp(gc)
        e_gm = jnp.exp(gc - lwc)
        e_ng = jnp.exp(-gc)
        e_end = jnp.exp(gc[CHUNK - 1:CHUNK, :])
        at = -kn_s[rows, :] * e_gm
        bt = b_s[rows, :] * e_ng
        kt = kp_s[rows, :] * e_ng
        rt = r_s[rows, :] * e_g
        vv = v_s[rows, :]
        for p in range(npair):
            sl = slice(p * LANES, (p + 1) * LANES)
            a_st = stack(at[:, sl])
            r_st = stack(rt[:, sl])
            b_st = stack(bt[:, sl])
            k_st = stack(kt[:, sl])
            v_st = stack(vv[:, sl])
            ee = e_end[:, sl]
            ar = jnp.concatenate([a_st, r_st], axis=0)
            bk = jnp.concatenate([b_st, k_st], axis=0)
            p4 = _mm(ar, bk, NT)
            a_ab = jnp.where(strict, p4[0:c2, 0:c2], 0.0)
            a_ak = jnp.where(strict, p4[0:c2, c2:2 * c2], 0.0)
            a_rb = jnp.where(incl, p4[c2:2 * c2, 0:c2], 0.0)
            a_rk = jnp.where(incl, p4[c2:2 * c2, c2:2 * c2], 0.0)
            t_inv = _inv_unit_lower(a_ab, eye, inv_passes)
            sp = state[p]
            xr = _mm(ar, sp, NT)
            xx = xr[0:c2] + _mm(a_ak, v_st)
            uu = _mm(t_inv, xx, pa=inv_passes)
            uv = jnp.concatenate([uu, v_st], axis=0)
            o_st = xr[c2:2 * c2] + _mm(jnp.concatenate([a_rb, a_rk], axis=1), uv)
            o_s[rows, sl] = o_st[0:CHUNK] + o_st[CHUNK:c2]
            bkd = bk * ee
            state[p] = sp * ee + _mm(uv.T, bkd)
        return carry

    lax.fori_loop(0, lb // CHUNK, chunk_body, 0)

    o = o_s[...]
    inv_n = 1.0 / RWKV_HEAD
    mean = _mm(o, ones_bd, pa=2) * inv_n
    d = o - mean
    var = _mm(d * d, ones_bd, pa=2) * inv_n
    on = d * lax.rsqrt(var + 1e-5 * RWKV_HEAD) * lnw_ref[...] + lnb_ref[...]
    o_ref[...] = (on + bonus) * g


def _rwkv(feats, prm, bsz, seq, lb, inv_passes):
    bl, fw = feats.shape
    width = prm['w0'].shape[1]
    row = lambda i, t: (0, 0)
    full = lambda a: pl.BlockSpec(a.shape, row)
    names = ['mu', 'w0', 'wup', 'a0', 'aup', 'gup', 'kk', 'ka', 'rk', 'lnw', 'lnb', 'ones']
    per_b = seq // lb
    kern = functools.partial(_rwkv_kernel, lb=lb, inv_passes=inv_passes)
    vm = lambda *s: pltpu.VMEM(s, F32)
    return pl.pallas_call(
        kern,
        grid=(bsz, per_b),
        in_specs=[pl.BlockSpec((lb, fw), lambda i, t: (i * per_b + t, 0))] + [full(prm[n]) for n in names],
        out_specs=pl.BlockSpec((lb, width), lambda i, t: (i * per_b + t, 0)),
        out_shape=jax.ShapeDtypeStruct((bl, width), F32),
        scratch_shapes=[vm(lb + SUBLANES, fw), vm(width // LANES, LANES, LANES)] + [vm(lb, width)] * 7,
        compiler_params=_cparams(("arbitrary", "arbitrary")),
        name="rwkv7",
    )(feats, *[prm[n] for n in names])


def _gdn_kernel(qkv_ref, ba_ref, cw_ref, alog_ref, dtb_ref, nw_ref, o_ref,
                cbuf, state, q_s, k_s, v_s, beta_s, ld_s, *, lb, nh, inv_passes):
    t = pl.program_id(1)
    hd = GDN_HEAD
    mixw = nh * hd

    @pl.when(t == 0)
    def _():
        cbuf[0:SUBLANES, :] = jnp.zeros((SUBLANES, cbuf.shape[1]), F32)
        state[...] = jnp.zeros_like(state)

    @pl.when(t > 0)
    def _():
        cbuf[0:SUBLANES, :] = cbuf[lb:lb + SUBLANES, :]

    cbuf[SUBLANES:SUBLANES + lb, :] = qkv_ref[...]

    def conv_silu(off):
        acc = None
        for j in range(GDN_CONV):
            term = cw_ref[j:j + 1, off:off + hd] * cbuf[pl.ds(SUBLANES - GDN_CONV + 1 + j, lb), off:off + hd]
            acc = term if acc is None else acc + term
        return _silu(acc)

    for h in range(nh):
        q = conv_silu(h * hd)
        k = conv_silu(mixw + h * hd)
        v = conv_silu(2 * mixw + h * hd)
        q_s[h] = q * lax.rsqrt(jnp.sum(q * q, axis=-1, keepdims=True) + 1e-6) * (hd ** -0.5)
        k_s[h] = k * lax.rsqrt(jnp.sum(k * k, axis=-1, keepdims=True) + 1e-6)
        v_s[h] = v

    ba = ba_ref[...]
    beta_s[...] = _sigmoid(ba)
    ld_s[...] = -jnp.exp(alog_ref[...]) * _softplus(ba + dtb_ref[...])

    c2 = 2 * CHUNK
    row = _iota2((c2, c2), 0)
    col = _iota2((c2, c2), 1)
    same = (row // CHUNK) == (col // CHUNK)
    strict = same & (row > col)
    incl = same & (row >= col)
    eye = jnp.where(row == col, 1.0, 0.0).astype(F32)
    tri = jnp.where(_iota2((CHUNK, CHUNK), 0) >= _iota2((CHUNK, CHUNK), 1), 1.0, 0.0).astype(BF16)
    lane_first = _iota2((1, c2), 1) < CHUNK

    def chunk_body(c, carry):
        r0 = pl.multiple_of(c * CHUNK, CHUNK)
        rows = pl.ds(r0, CHUNK)
        g = _mm(tri, ld_s[rows, :], pb=3)
        g2t = jnp.concatenate([g, g], axis=0).T
        beta = beta_s[rows, :]
        for hp in range(nh // 2):
            h0 = 2 * hp
            h1 = h0 + 1
            gcol = jnp.concatenate([g[:, nh + h0:nh + h0 + 1], g[:, nh + h1:nh + h1 + 1]], axis=0)
            grow = jnp.where(lane_first, g2t[nh + h0:nh + h0 + 1, :], g2t[nh + h1:nh + h1 + 1, :])
            bcol = jnp.concatenate([beta[:, h0:h0 + 1], beta[:, h1:h1 + 1]], axis=0)
            diff = gcol - grow
            dec = jnp.where(incl, jnp.exp(jnp.where(incl, diff, 0.0)), 0.0)
            e_g = jnp.exp(gcol)
            g_end = jnp.concatenate([jnp.broadcast_to(g[CHUNK - 1:CHUNK, nh + h0:nh + h0 + 1], (CHUNK, 1)),
                                     jnp.broadcast_to(g[CHUNK - 1:CHUNK, nh + h1:nh + h1 + 1], (CHUNK, 1))], axis=0)
            q_st = jnp.concatenate([q_s[h0, rows, :], q_s[h1, rows, :]], axis=0)
            k_st = jnp.concatenate([k_s[h0, rows, :], k_s[h1, rows, :]], axis=0)
            v_st = jnp.concatenate([v_s[h0, rows, :], v_s[h1, rows, :]], axis=0)
            kb = k_st * bcol
            a_mat = jnp.where(strict, _mm(kb, k_st, NT) * dec, 0.0)
            intra = jnp.where(incl, _mm(q_st, k_st, NT) * dec, 0.0)
            t_inv = _inv_unit_lower(-a_mat, eye, inv_passes)
            uw = _mm(t_inv, jnp.concatenate([v_st * bcol, kb * e_g], axis=1), pa=inv_passes)
            u_st = uw[:, 0:hd]
            w_st = uw[:, hd:2 * hd]
            qd = q_st * e_g
            k_end = k_st * jnp.exp(g_end - gcol)
            vn = []
            oq = []
            for i, h in enumerate((h0, h1)):
                hs = slice(i * CHUNK, (i + 1) * CHUNK)
                s_h = state[h]
                wq = _mm(jnp.concatenate([w_st[hs], qd[hs]], axis=0), s_h)
                vn.append(u_st[hs] - wq[0:CHUNK])
                oq.append(wq[CHUNK:c2])
            vn_st = jnp.concatenate(vn, axis=0)
            o_st = jnp.concatenate(oq, axis=0) + _mm(intra, vn_st)
            for i, h in enumerate((h0, h1)):
                hs = slice(i * CHUNK, (i + 1) * CHUNK)
                cd = jnp.exp(g[CHUNK - 1:CHUNK, nh + h:nh + h + 1])
                state[h] = state[h] * cd + _mm(k_end[hs].T, vn_st[hs])
                o_h = o_st[hs]
                o_h = o_h * lax.rsqrt(jnp.mean(o_h * o_h, axis=-1, keepdims=True) + NORM_EPS) * nw_ref[...]
                o_ref[rows, h * hd:(h + 1) * hd] = o_h
        return carry

    lax.fori_loop(0, lb // CHUNK, chunk_body, 0)


def _gdn(qkv, ba, prm, bsz, seq, lb, inv_passes):
    bl, w3 = qkv.shape
    mixw = w3 // 3
    nh = mixw // GDN_HEAD
    per_b = seq // lb
    full = lambda a: pl.BlockSpec(a.shape, lambda i, t: (0, 0))
    kern = functools.partial(_gdn_kernel, lb=lb, nh=nh, inv_passes=inv_passes)
    vm = lambda *s: pltpu.VMEM(s, F32)
    names = ['cw', 'alog', 'dtb', 'nw']
    return pl.pallas_call(
        kern,
        grid=(bsz, per_b),
        in_specs=[pl.BlockSpec((lb, w3), lambda i, t: (i * per_b + t, 0)),
                  pl.BlockSpec((lb, LANES), lambda i, t: (i * per_b + t, 0))] + [full(prm[n]) for n in names],
        out_specs=pl.BlockSpec((lb, mixw), lambda i, t: (i * per_b + t, 0)),
        out_shape=jax.ShapeDtypeStruct((bl, mixw), F32),
        scratch_shapes=[vm(lb + SUBLANES, w3), vm(nh, GDN_HEAD, GDN_HEAD),
                        vm(nh, lb, GDN_HEAD), vm(nh, lb, GDN_HEAD), vm(nh, lb, GDN_HEAD),
                        vm(lb, LANES), vm(lb, LANES)],
        compiler_params=_cparams(("arbitrary", "arbitrary")),
        name="gdn",
    )(qkv, ba, *[prm[n] for n in names])


def _finish(x_ref, g_ref, acc, fw_ref, o_ref, final):
    xn = x_ref[...] + g_ref[0] * acc
    if final:
        ms = jnp.mean(xn * xn, axis=-1, keepdims=True)
        xn = xn * lax.rsqrt(ms + NORM_EPS) * fw_ref[...]
    o_ref[...] = xn


def _outproj_even_kernel(conv_ref, u_ref, yb_ref, z_ref, x_ref, g_ref, d_ref, gw_ref, gb_ref, w_ref, fw_ref,
                         o_ref, *, final):
    nj = conv_ref.shape[0]
    ya = jnp.concatenate([conv_ref[j] for j in range(nj)], axis=1)
    uu = jnp.concatenate([u_ref[j] for j in range(nj)], axis=1)
    ya = _gelu_tanh(ya + d_ref[...] * uu)
    ya = ya * _sigmoid(_mm(ya, gw_ref[...]) + gb_ref[...])
    wa = ya.shape[1]
    sz = _silu(z_ref[...])
    acc = _mm(ya * sz[:, :wa], w_ref[0:wa, :]) + _mm(yb_ref[...] * sz[:, wa:], w_ref[wa:, :])
    _finish(x_ref, g_ref, acc, fw_ref, o_ref, final)


def _outproj_odd_kernel(y_ref, z_ref, x_ref, g_ref, w_ref, fw_ref, o_ref, *, final):
    acc = _mm(y_ref[...] * _silu(z_ref[...]), w_ref[...])
    _finish(x_ref, g_ref, acc, fw_ref, o_ref, final)


def kernel(x, c, norm_w, ada_w, ada_b, w_out, final_norm_w, even_w_in, s5_lambda_re, s5_lambda_im, s5_log_step, s5_b_re, s5_b_im, s5_c_re, s5_c_im, s5_d, s5_glu_w, s5_glu_b, rwkv_mu, rwkv_w0, rwkv_w_up, rwkv_a0, rwkv_a_up, rwkv_g_up, rwkv_k_k, rwkv_k_a, rwkv_r_k, rwkv_ln_w, rwkv_ln_b, odd_w_in, gdn_conv_w, gdn_a_log, gdn_dt_bias, gdn_norm_w):
    bsz, seq, d = x.shape
    depth = norm_w.shape[0]
    bl = bsz * seq
    tm = min(512, seq)
    lb = min(256, seq)
    inv_passes = 2
    s5w = s5_d.shape[1]
    nj = s5w // LANES
    rw = rwkv_w0.shape[1]
    mixw = w_out.shape[1]
    nh_gdn = mixw // GDN_HEAD
    per_b = seq // tm

    mod = _modulation(c, ada_w, ada_b)
    x2 = x.reshape(bl, d)
    row1 = lambda a: a.reshape(1, -1)
    tile_spec = lambda w: pl.BlockSpec((tm, w), lambda i: (i, 0))
    res_spec = lambda a: pl.BlockSpec(a.shape, lambda i: tuple(0 for _ in a.shape))
    gate_spec = pl.BlockSpec((1, 1, d), lambda i: (i // per_b, 0, 0))
    fw = row1(final_norm_w)

    for layer in range(depth):
        shift = mod[layer, :, 0:d].reshape(bsz, 1, d)
        scale = mod[layer, :, d:2 * d].reshape(bsz, 1, d)
        gate = mod[layer, :, 2 * d:3 * d].reshape(bsz, 1, d)
        i = layer // 2
        final = layer == depth - 1
        nw = row1(norm_w[layer])
        wo = w_out[layer].astype(BF16)
        if layer % 2 == 0:
            fwid = even_w_in.shape[2] - s5w - mixw
            u4, feats, z = _inproj(
                x2, nw, shift, scale, even_w_in[i].astype(BF16),
                functools.partial(_inproj_even_kernel, tn=512),
                (jax.ShapeDtypeStruct((nj, bl, LANES), F32), jax.ShapeDtypeStruct((bl, fwid), F32),
                 jax.ShapeDtypeStruct((bl, mixw), F32)),
                (pl.BlockSpec((nj, tm, LANES), lambda r: (0, r, 0)), tile_spec(fwid), tile_spec(mixw)),
                seq, tm)
            conv4 = _s5_scan(u4, _s5_weights(s5_lambda_re[i], s5_lambda_im[i], s5_log_step[i], s5_b_re[i],
                                             s5_b_im[i], s5_c_re[i], s5_c_im[i]), bsz, seq)
            zpad = jnp.zeros((LANES - DECAY_LORA, rw), F32)
            ones_bd = jnp.kron(jnp.eye(rw // RWKV_HEAD, dtype=F32), jnp.ones((RWKV_HEAD, RWKV_HEAD), F32))
            prm = dict(mu=row1(rwkv_mu[i]), w0=row1(rwkv_w0[i]),
                       wup=jnp.concatenate([rwkv_w_up[i], zpad], axis=0).astype(BF16),
                       a0=row1(rwkv_a0[i]),
                       aup=jnp.concatenate([zpad, rwkv_a_up[i]], axis=0).astype(BF16),
                       gup=rwkv_g_up[i].astype(BF16), kk=row1(rwkv_k_k[i]), ka=row1(rwkv_k_a[i]),
                       rk=row1(rwkv_r_k[i]), lnw=row1(rwkv_ln_w[i]), lnb=row1(rwkv_ln_b[i]),
                       ones=ones_bd.astype(BF16))
            yb = _rwkv(feats, prm, bsz, seq, lb, inv_passes)
            ins = (conv4, u4, yb, z, x2, gate, row1(s5_d[i]), s5_glu_w[i].astype(BF16), row1(s5_glu_b[i]), wo, fw)
            specs = [pl.BlockSpec((nj, tm, LANES), lambda r: (0, r, 0)),
                     pl.BlockSpec((nj, tm, LANES), lambda r: (0, r, 0)),
                     tile_spec(rw), tile_spec(mixw), tile_spec(d), gate_spec] + [res_spec(a) for a in ins[6:]]
            kern = functools.partial(_outproj_even_kernel, final=final)
        else:
            w_in = odd_w_in[i]
            q_end = 3 * mixw
            ba_w = jnp.concatenate([w_in[:, q_end:q_end + 2 * nh_gdn],
                                    jnp.zeros((d, LANES - 2 * nh_gdn), F32)], axis=1)
            w_r = jnp.concatenate([w_in[:, :q_end], w_in[:, q_end + 2 * nh_gdn:], ba_w], axis=1).astype(BF16)
            qkv, z, ba = _inproj(
                x2, nw, shift, scale, w_r,
                functools.partial(_inproj_odd_kernel, tn=512),
                (jax.ShapeDtypeStruct((bl, q_end), F32), jax.ShapeDtypeStruct((bl, mixw), F32),
                 jax.ShapeDtypeStruct((bl, LANES), F32)),
                (tile_spec(q_end), tile_spec(mixw), tile_spec(LANES)),
                seq, tm)
            pad_row = lambda a: jnp.zeros((1, LANES), F32).at[0, nh_gdn:2 * nh_gdn].set(a)
            prm = dict(cw=gdn_conv_w[i], alog=pad_row(gdn_a_log[i]), dtb=pad_row(gdn_dt_bias[i]),
                       nw=row1(gdn_norm_w[i]))
            y = _gdn(qkv, ba, prm, bsz, seq, lb, inv_passes)
            ins = (y, z, x2, gate, wo, fw)
            specs = [tile_spec(mixw), tile_spec(mixw), tile_spec(d), gate_spec] + [res_spec(a) for a in ins[4:]]
            kern = functools.partial(_outproj_odd_kernel, final=final)
        x2 = pl.pallas_call(
            kern,
            grid=(bl // tm,),
            in_specs=specs,
            out_specs=tile_spec(d),
            out_shape=jax.ShapeDtypeStruct((bl, d), F32),
            compiler_params=_cparams(("arbitrary",)),
            name="out_proj",
        )(*ins)
    return x2.reshape(bsz, seq, d)
```

```python
import functools
import math

import jax
import jax.numpy as jnp
from jax import lax
from jax.experimental import pallas as pl
from jax.experimental.pallas import tpu as pltpu

F32 = jnp.float32
BF16 = jnp.bfloat16

NORM_EPS = 1e-6
LANES = 128
SUBLANES = 8
VMEM_LIMIT = 56 * 1024 * 1024

S5_GROUP = 16
S5_STATE = 64
RWKV_HEAD = 64
DECAY_LORA = 64
ICLR_LORA = 64
GATE_LORA = 128
RWKV_DECAY_SCALE = math.exp(-0.5)
GDN_HEAD = 128
GDN_CONV = 4
CHUNK = 64
S5_T = 8

NN = (((1,), (0,)), ((), ()))
NT = (((1,), (1,)), ((), ()))
BNN = (((2,), (1,)), ((0,), (0,)))
BNT = (((2,), (2,)), ((0,), (0,)))


def _split(a, n):
    if a.dtype == BF16:
        return [a]
    parts = []
    r = a
    for i in range(n):
        p = r.astype(BF16)
        parts.append(p)
        if i + 1 < n:
            r = r - p.astype(F32)
    return parts


def _mm(a, b, dn=NN, pa=1, pb=1):
    ap = _split(a, pa)
    bp = _split(b, pb)
    lim = max(len(ap), len(bp))
    acc = None
    for i, x in enumerate(ap):
        for j, y in enumerate(bp):
            if i + j >= lim:
                continue
            t = lax.dot_general(x, y, dn, preferred_element_type=F32)
            acc = t if acc is None else acc + t
    return acc


def _sigmoid(x):
    return 1.0 / (1.0 + jnp.exp(-x))


def _silu(x):
    return x * _sigmoid(x)


def _softplus(x):
    return jnp.maximum(x, 0.0) + jnp.log(1.0 + jnp.exp(-jnp.abs(x)))


def _gelu_tanh(x):
    c = math.sqrt(2.0 / math.pi)
    return 0.5 * x * (1.0 + jnp.tanh(c * (x + 0.044715 * (x * x * x))))


def _iota2(shape, dim):
    return lax.broadcasted_iota(jnp.int32, shape, dim)


def _inv_unit_lower(n_mat, eye, passes, dn):
    t = eye + n_mat
    p = n_mat
    steps = int(math.log2(CHUNK)) - 1
    for _ in range(steps):
        p = _mm(p, p, dn, pa=passes, pb=passes)
        t = t + _mm(t, p, dn, pa=passes, pb=passes)
    return t


def _cparams(sem):
    return pltpu.CompilerParams(dimension_semantics=sem, vmem_limit_bytes=VMEM_LIMIT)


def _mod_kernel(c_ref, w_ref, b_ref, o_ref):
    s = _silu(c_ref[...])
    o_ref[0] = _mm(s, w_ref[0], pa=2, pb=2) + b_ref[0]


def _modulation(c, ada_w, ada_b):
    depth, d, d3 = ada_w.shape
    bsz = c.shape[0]
    nj = d3 // d
    return pl.pallas_call(
        _mod_kernel,
        grid=(depth, nj),
        in_specs=[
            pl.BlockSpec((bsz, d), lambda l, j: (0, 0)),
            pl.BlockSpec((1, d, d), lambda l, j: (l, 0, j)),
            pl.BlockSpec((1, 1, d), lambda l, j: (l, 0, j)),
        ],
        out_specs=pl.BlockSpec((1, bsz, d), lambda l, j: (l, 0, j)),
        out_shape=jax.ShapeDtypeStruct((depth, bsz, d3), F32),
        compiler_params=_cparams(("arbitrary", "arbitrary")),
        name="adaln_mod",
    )(c, ada_w, ada_b.reshape(depth, 1, d3))


def _norm_mod(x, nw, scale, shift):
    ms = jnp.mean(x * x, axis=-1, keepdims=True)
    h = x * lax.rsqrt(ms + NORM_EPS) * nw
    return h * (1.0 + scale) + shift


def _inproj_even_kernel(x_ref, nw_ref, sh_ref, sc_ref, w_ref, u_ref, f_ref, z_ref, us_ref, *, tn):
    hb = _norm_mod(x_ref[...], nw_ref[...], sc_ref[0], sh_ref[0]).astype(BF16)
    nu = u_ref.shape[0]
    nrow = u_ref.shape[2]
    for j in range(nu):
        us_ref[j] = _mm(hb, w_ref[:, j * LANES:(j + 1) * LANES])
        for t in range(S5_T):
            u_ref[j, 0, :, t * LANES:(t + 1) * LANES] = us_ref[j, pl.ds(t, nrow, stride=S5_T), :]
    off = nu * LANES
    for o_ref in (f_ref, z_ref):
        width = o_ref.shape[1]
        for j0 in range(0, width, tn):
            w = min(tn, width - j0)
            o_ref[:, j0:j0 + w] = _mm(hb, w_ref[:, off + j0:off + j0 + w])
        off += width


def _inproj_odd_kernel(x_ref, nw_ref, sh_ref, sc_ref, w_ref, qkv_ref, z_ref, ba_ref, *, tn):
    hb = _norm_mod(x_ref[...], nw_ref[...], sc_ref[0], sh_ref[0]).astype(BF16)
    off = 0
    for o_ref in (qkv_ref, z_ref, ba_ref):
        width = o_ref.shape[1]
        for j0 in range(0, width, tn):
            w = min(tn, width - j0)
            o_ref[:, j0:j0 + w] = _mm(hb, w_ref[:, off + j0:off + j0 + w])
        off += width


def _inproj(x2, nw, shift, scale, w_bf, kernel_fn, out_shapes, out_specs, seq, tm, scratch=()):
    bl, d = x2.shape
    n = w_bf.shape[1]
    per_b = seq // tm
    return pl.pallas_call(
        kernel_fn,
        grid=(bl // tm,),
        in_specs=[
            pl.BlockSpec((tm, d), lambda i: (i, 0)),
            pl.BlockSpec((1, d), lambda i: (0, 0)),
            pl.BlockSpec((1, 1, d), lambda i: (i // per_b, 0, 0)),
            pl.BlockSpec((1, 1, d), lambda i: (i // per_b, 0, 0)),
            pl.BlockSpec((d, n), lambda i: (0, 0)),
        ],
        out_specs=out_specs,
        out_shape=out_shapes,
        scratch_shapes=list(scratch),
        compiler_params=_cparams(("arbitrary",)),
        name="in_proj",
    )(x2, nw, shift, scale, w_bf)


def _s5_kernel(u_ref, m_ref, p_ref, q_ref, are_ref, aim_ref, y_ref, st_ref, sloc_ref, sprev_ref, *, tc, nb):
    cb = pl.program_id(1)

    @pl.when(cb == 0)
    def _():
        st_ref[...] = jnp.zeros_like(st_ref)

    kcols = u_ref.shape[-1]
    x = u_ref[0].reshape(nb * tc, kcols).astype(BF16)
    y = _mm(x, m_ref[0])
    sloc = _mm(x, p_ref[0])
    nt2 = sloc_ref.shape[0]
    nt = nt2 // 2
    for k in range(nt2):
        sloc_ref[k] = sloc[:, k * LANES:(k + 1) * LANES]
    a_re = are_ref[0]
    a_im = aim_ref[0]

    def body(c, carry):
        rows = pl.ds(c, nb, stride=tc)
        new = []
        for k in range(nt):
            s_re, s_im = carry[k], carry[nt + k]
            sprev_ref[k, rows, :] = s_re
            sprev_ref[nt + k, rows, :] = s_im
            ar = a_re[:, k * LANES:(k + 1) * LANES]
            ai = a_im[:, k * LANES:(k + 1) * LANES]
            new.append((ar * s_re - ai * s_im + sloc_ref[k, rows, :],
                        ar * s_im + ai * s_re + sloc_ref[nt + k, rows, :]))
        return tuple(n[0] for n in new) + tuple(n[1] for n in new)

    st = lax.fori_loop(0, tc, body, tuple(st_ref[k] for k in range(nt2)))
    for k in range(nt2):
        st_ref[k] = st[k]
    sprev = jnp.concatenate([sprev_ref[k] for k in range(nt2)], axis=1)
    y = y + _mm(sprev, q_ref[0])
    y_ref[0] = y.reshape(nb, tc, kcols)


def _s5_weights(lambda_re, lambda_im, log_step, b_re, b_im, c_re, c_im):
    g_all, p_all = lambda_re.shape
    t_len = S5_T
    gpt = LANES // S5_GROUP
    nj = g_all // gpt
    lam = lax.complex(jnp.minimum(lambda_re, -1e-4), lambda_im)
    step = jnp.exp(log_step)[:, None]
    lam_bar = jnp.exp(lam * step)
    b_bar = ((lam_bar - 1.0) / lam)[..., None] * lax.complex(b_re, b_im)
    c_mat = lax.complex(c_re, c_im)
    taus = jnp.arange(t_len + 1, dtype=F32)
    pw = jnp.exp((lam * step)[None] * taus[:, None, None])
    hi = lax.Precision.HIGHEST
    kern = jnp.real(jnp.einsum('gcp,tgp,gpd->tgcd', c_mat, pw[:t_len], b_bar, precision=hi))
    s_idx = jnp.arange(t_len)
    lag = s_idx[None, :] - s_idx[:, None]
    kt = kern[jnp.clip(lag, 0, t_len - 1)]
    kt = jnp.where((lag >= 0)[:, :, None, None, None], kt, 0.0)
    kt = kt.reshape(t_len, t_len, nj, gpt, S5_GROUP, S5_GROUP)
    eye_g = jnp.eye(gpt, dtype=F32)
    m_mat = jnp.einsum('stjgoi,hg->jshitgo', kt, eye_g).reshape(nj, t_len * LANES, t_len * LANES)
    pb = pw[t_len - 1 - s_idx][:, :, :, None] * b_bar[None]
    pb = jnp.stack([jnp.real(pb), jnp.imag(pb)], axis=0)
    pb = pb.reshape(2, t_len, nj, gpt, p_all, S5_GROUP)
    p_mat = jnp.einsum('rsjgpi,hg->jshirgp', pb, eye_g).reshape(nj, t_len * LANES, 2 * gpt * p_all)
    cq = c_mat[None] * pw[1:t_len + 1][:, :, None, :]
    cq = jnp.stack([jnp.real(cq), -jnp.imag(cq)], axis=0)
    cq = cq.reshape(2, t_len, nj, gpt, S5_GROUP, p_all)
    q_mat = jnp.einsum('rtjgop,hg->jrhptgo', cq, eye_g).reshape(nj, 2 * gpt * p_all, t_len * LANES)
    a_t = pw[t_len].reshape(nj, 1, gpt * p_all)
    return (m_mat.astype(BF16), p_mat.astype(BF16), q_mat.astype(BF16),
            jnp.real(a_t).astype(F32), jnp.imag(a_t).astype(F32))


def _s5_scan(u5, weights):
    m_mat, p_mat, q_mat, a_re, a_im = weights
    nj, bsz, nchunks, kcols = u5.shape
    tc = min(64, nchunks)
    ns2 = p_mat.shape[-1]
    kern = functools.partial(_s5_kernel, tc=tc, nb=bsz)
    return pl.pallas_call(
        kern,
        grid=(nj, nchunks // tc),
        in_specs=[
            pl.BlockSpec((1, bsz, tc, kcols), lambda j, c: (j, 0, c, 0)),
            pl.BlockSpec((1, kcols, kcols), lambda j, c: (j, 0, 0)),
            pl.BlockSpec((1, kcols, ns2), lambda j, c: (j, 0, 0)),
            pl.BlockSpec((1, ns2, kcols), lambda j, c: (j, 0, 0)),
            pl.BlockSpec((1, 1, ns2 // 2), lambda j, c: (j, 0, 0)),
            pl.BlockSpec((1, 1, ns2 // 2), lambda j, c: (j, 0, 0)),
        ],
        out_specs=pl.BlockSpec((1, bsz, tc, kcols), lambda j, c: (j, 0, c, 0)),
        out_shape=jax.ShapeDtypeStruct((nj, bsz, nchunks, kcols), F32),
        scratch_shapes=[
            pltpu.VMEM((ns2 // LANES, bsz, LANES), F32),
            pltpu.VMEM((ns2 // LANES, bsz * tc, LANES), F32),
            pltpu.VMEM((ns2 // LANES, bsz * tc, LANES), F32),
        ],
        compiler_params=_cparams(("arbitrary", "arbitrary")),
        name="s5_scan",
    )(u5, m_mat, p_mat, q_mat, a_re, a_im)


def _rwkv_kernel(f_ref, mu_ref, w0_ref, wup_ref, a0_ref, aup_ref, gup_ref, kk_ref, ka_ref, rk_ref,
                 lnw_ref, lnb_ref, ones_ref, o_ref,
                 fbuf, state, a_st, r_st, b_st, k_st, v_st, arw_st, tav_st, rkv_st, arb_st, bkd_st, vk_st,
                 ee_st, o_s, *, lb, inv_passes):
    t = pl.program_id(1)
    width = o_s.shape[1]
    npair = width // LANES
    nc = lb // CHUNK
    c2 = 2 * CHUNK

    @pl.when(t == 0)
    def _():
        fbuf[0:SUBLANES, :] = jnp.zeros((SUBLANES, fbuf.shape[1]), F32)
        state[...] = jnp.zeros_like(state)

    @pl.when(t > 0)
    def _():
        fbuf[0:SUBLANES, :] = fbuf[lb:lb + SUBLANES, :]

    x = f_ref[...]
    fbuf[SUBLANES:SUBLANES + lb, :] = x
    xs = fbuf[pl.ds(SUBLANES - 1, lb), :]
    f = x + mu_ref[...] * (xs - x)
    r = f[:, 0:width]
    k = f[:, width:2 * width]
    v = f[:, 2 * width:3 * width]
    lora = f[:, 3 * width:3 * width + LANES]
    xg = f[:, 3 * width + LANES:3 * width + 2 * LANES]
    lw = -RWKV_DECAY_SCALE * _sigmoid(w0_ref[...] + _mm(jnp.tanh(lora), wup_ref[...]))
    a = _sigmoid(a0_ref[...] + _mm(lora, aup_ref[...]))
    g = _mm(_sigmoid(xg), gup_ref[...])
    ones_bd = ones_ref[...]
    kk = k * kk_ref[...]
    kn = kk * lax.rsqrt(_mm(kk * kk, ones_bd) + 1e-6)
    kp = k * (1.0 + (a - 1.0) * ka_ref[...])
    bonus = _mm(r * kp * rk_ref[...], ones_bd) * v
    bb = kn * a

    row = _iota2((c2, c2), 0)
    col = _iota2((c2, c2), 1)
    same = (row // CHUNK) == (col // CHUNK)
    strict = same & (row > col)
    incl = same & (row >= col)
    eye = jnp.where(row == col, 1.0, 0.0).astype(F32)
    tri = jnp.where(_iota2((CHUNK, CHUNK), 0) >= _iota2((CHUNK, CHUNK), 1), 1.0, 0.0).astype(BF16)
    first = _iota2((CHUNK, LANES), 1) < RWKV_HEAD

    def stack(m):
        return jnp.concatenate([jnp.where(first, m, 0.0), jnp.where(first, 0.0, m)], axis=0)

    for c in range(nc):
        cs = slice(c * CHUNK, (c + 1) * CHUNK)
        lwc = lw[cs]
        gc = _mm(tri, lwc, pb=3)
        e_g = jnp.exp(gc)
        e_ng = jnp.exp(-gc)
        at = -kn[cs] * jnp.exp(gc - lwc)
        bt = bb[cs] * e_ng
        kt = kp[cs] * e_ng
        rt = r[cs] * e_g
        vv = v[cs]
        e_end = jnp.broadcast_to(jnp.exp(gc[CHUNK - 1:CHUNK, :]), (SUBLANES, width))
        for p in range(npair):
            idx = c * npair + p
            sl = slice(p * LANES, (p + 1) * LANES)
            a_st[idx] = stack(at[:, sl])
            r_st[idx] = stack(rt[:, sl])
            b_st[idx] = stack(bt[:, sl])
            k_st[idx] = stack(kt[:, sl])
            v_st[idx] = stack(vv[:, sl])
            ee_st[idx] = e_end[:, sl]

    a_all = a_st[...]
    r_all = r_st[...]
    v_all = v_st[...]
    ar = jnp.concatenate([a_all, r_all], axis=1)
    bk = jnp.concatenate([b_st[...], k_st[...]], axis=1)
    p4 = _mm(ar, bk, BNT)
    a_ab = jnp.where(strict, p4[:, 0:c2, 0:c2], 0.0)
    a_ak = jnp.where(strict, p4[:, 0:c2, c2:2 * c2], 0.0)
    a_rb = jnp.where(incl, p4[:, c2:2 * c2, 0:c2], 0.0)
    a_rk = jnp.where(incl, p4[:, c2:2 * c2, c2:2 * c2], 0.0)
    t_inv = _inv_unit_lower(a_ab, eye, inv_passes, BNN)
    arw_st[...] = jnp.concatenate([_mm(t_inv, a_all, BNN, pa=inv_passes), r_all], axis=1)
    tav_st[...] = _mm(t_inv, _mm(a_ak, v_all, BNN), BNN, pa=inv_passes)
    rkv_st[...] = _mm(a_rk, v_all, BNN)
    arb_st[...] = a_rb
    bkd = bk * ee_st[:, 0:1, :]
    bkd_st[...] = bkd[:, 0:c2, :]
    vk_st[...] = _mm(jnp.swapaxes(v_all, 1, 2), bkd[:, c2:2 * c2, :], BNN)

    def chunk_body(c, carry):
        rows = pl.ds(pl.multiple_of(c * CHUNK, CHUNK), CHUNK)
        xr = [_mm(arw_st[c * npair + p], state[p], NT) for p in range(npair)]
        uu = [xr[p][0:c2] + tav_st[c * npair + p] for p in range(npair)]
        ob = [_mm(arb_st[c * npair + p], uu[p]) for p in range(npair)]
        for p in range(npair):
            idx = c * npair + p
            o_st = xr[p][c2:2 * c2] + ob[p] + rkv_st[idx]
            o_s[rows, p * LANES:(p + 1) * LANES] = o_st[0:CHUNK] + o_st[CHUNK:c2]
            state[p] = state[p] * ee_st[idx, 0:1, :] + vk_st[idx] + _mm(uu[p].T, bkd_st[idx])
        return carry

    lax.fori_loop(0, nc, chunk_body, 0)

    o = o_s[...]
    inv_n = 1.0 / RWKV_HEAD
    mean = _mm(o, ones_bd) * inv_n
    d = o - mean
    var = _mm(d * d, ones_bd) * inv_n
    on = d * lax.rsqrt(var + 1e-5 * RWKV_HEAD) * lnw_ref[...] + lnb_ref[...]
    o_ref[...] = (on + bonus) * g


def _rwkv(feats, prm, bsz, seq, lb, inv_passes):
    bl, fw = feats.shape
    width = prm['w0'].shape[1]
    full = lambda a: pl.BlockSpec(a.shape, lambda i, t: (0, 0))
    names = ['mu', 'w0', 'wup', 'a0', 'aup', 'gup', 'kk', 'ka', 'rk', 'lnw', 'lnb', 'ones']
    per_b = seq // lb
    npair = width // LANES
    n = (lb // CHUNK) * npair
    c2 = 2 * CHUNK
    kern = functools.partial(_rwkv_kernel, lb=lb, inv_passes=inv_passes)
    vm = lambda *s: pltpu.VMEM(s, F32)
    return pl.pallas_call(
        kern,
        grid=(bsz, per_b),
        in_specs=[pl.BlockSpec((lb, fw), lambda i, t: (i * per_b + t, 0))] + [full(prm[k]) for k in names],
        out_specs=pl.BlockSpec((lb, width), lambda i, t: (i * per_b + t, 0)),
        out_shape=jax.ShapeDtypeStruct((bl, width), F32),
        scratch_shapes=[vm(lb + SUBLANES, fw), vm(npair, LANES, LANES)]
                       + [vm(n, c2, LANES)] * 5
                       + [vm(n, 2 * c2, LANES), vm(n, c2, LANES), vm(n, c2, LANES), vm(n, c2, c2),
                          vm(n, c2, LANES), vm(n, LANES, LANES), vm(n, SUBLANES, LANES), vm(lb, width)],
        compiler_params=_cparams(("arbitrary", "arbitrary")),
        name="rwkv7",
    )(feats, *[prm[k] for k in names])


def _gdn_kernel(qkv_ref, ba_ref, cw_ref, alog_ref, dtb_ref, nw_ref, o_ref,
                cbuf, state, q_st, k_st, v_st, u_st, w_st, qd_st, ket_st, in_st, cd_st, *, lb, nh, inv_passes):
    t = pl.program_id(1)
    hd = GDN_HEAD
    mixw = nh * hd
    nc = lb // CHUNK
    npair = nh // 2
    c2 = 2 * CHUNK

    @pl.when(t == 0)
    def _():
        cbuf[0:SUBLANES, :] = jnp.zeros((SUBLANES, cbuf.shape[1]), F32)
        state[...] = jnp.zeros_like(state)

    @pl.when(t > 0)
    def _():
        cbuf[0:SUBLANES, :] = cbuf[lb:lb + SUBLANES, :]

    cbuf[SUBLANES:SUBLANES + lb, :] = qkv_ref[...]

    def conv_silu(off):
        acc = None
        for j in range(GDN_CONV):
            term = cw_ref[j:j + 1, off:off + hd] * cbuf[pl.ds(SUBLANES - GDN_CONV + 1 + j, lb), off:off + hd]
            acc = term if acc is None else acc + term
        return _silu(acc)

    for h in range(nh):
        q = conv_silu(h * hd)
        k = conv_silu(mixw + h * hd)
        v = conv_silu(2 * mixw + h * hd)
        q = q * lax.rsqrt(jnp.sum(q * q, axis=-1, keepdims=True) + 1e-6) * (hd ** -0.5)
        k = k * lax.rsqrt(jnp.sum(k * k, axis=-1, keepdims=True) + 1e-6)
        hp, i = divmod(h, 2)
        rs = slice(i * CHUNK, (i + 1) * CHUNK)
        for c in range(nc):
            cs = slice(c * CHUNK, (c + 1) * CHUNK)
            q_st[c * npair + hp, rs, :] = q[cs]
            k_st[c * npair + hp, rs, :] = k[cs]
            v_st[c * npair + hp, rs, :] = v[cs]

    ba = ba_ref[...]
    beta = _sigmoid(ba)
    ld = -jnp.exp(alog_ref[...]) * _softplus(ba + dtb_ref[...])

    row = _iota2((c2, c2), 0)
    col = _iota2((c2, c2), 1)
    same = (row // CHUNK) == (col // CHUNK)
    strict = same & (row > col)
    incl = same & (row >= col)
    eye = jnp.where(row == col, 1.0, 0.0).astype(F32)
    tri = jnp.where(_iota2((CHUNK, CHUNK), 0) >= _iota2((CHUNK, CHUNK), 1), 1.0, 0.0).astype(BF16)
    lane_first = _iota2((1, c2), 1) < CHUNK
    row_first = _iota2((c2, hd), 0) < CHUNK

    gcols, grows, bcols, gends = [], [], [], []
    for c in range(nc):
        cs = slice(c * CHUNK, (c + 1) * CHUNK)
        g = _mm(tri, ld[cs], pb=3)
        g2t = jnp.concatenate([g, g], axis=0).T
        bc = beta[cs]
        for hp in range(npair):
            l0 = nh + 2 * hp
            l1 = l0 + 1
            gcols.append(jnp.concatenate([g[:, l0:l0 + 1], g[:, l1:l1 + 1]], axis=0))
            grows.append(jnp.where(lane_first, g2t[l0:l0 + 1, :], g2t[l1:l1 + 1, :]))
            bcols.append(jnp.concatenate([bc[:, 2 * hp:2 * hp + 1], bc[:, 2 * hp + 1:2 * hp + 2]], axis=0))
            gends.append(jnp.concatenate([jnp.broadcast_to(g[CHUNK - 1:CHUNK, l0:l0 + 1], (CHUNK, 1)),
                                          jnp.broadcast_to(g[CHUNK - 1:CHUNK, l1:l1 + 1], (CHUNK, 1))], axis=0))
    gcol = jnp.stack(gcols)
    grow = jnp.stack(grows)
    bcol = jnp.stack(bcols)
    gend = jnp.stack(gends)

    q_all = q_st[...]
    k_all = k_st[...]
    v_all = v_st[...]
    dec = jnp.where(incl, jnp.exp(jnp.where(incl, gcol - grow, 0.0)), 0.0)
    e_g = jnp.exp(gcol)
    kb = k_all * bcol
    a_mat = jnp.where(strict, _mm(kb, k_all, BNT) * dec, 0.0)
    in_st[...] = jnp.where(incl, _mm(q_all, k_all, BNT) * dec, 0.0)
    t_inv = _inv_unit_lower(-a_mat, eye, inv_passes, BNN)
    uw = _mm(t_inv, jnp.concatenate([v_all * bcol, kb * e_g], axis=2), BNN, pa=inv_passes)
    u_st[...] = uw[:, :, 0:hd]
    w_st[...] = uw[:, :, hd:2 * hd]
    qd_st[...] = q_all * e_g
    ket_st[...] = jnp.swapaxes(k_all * jnp.exp(gend - gcol), 1, 2)
    cd_st[...] = jnp.broadcast_to(jnp.exp(gend), cd_st.shape)

    def chunk_body(c, carry):
        rows = pl.ds(pl.multiple_of(c * CHUNK, CHUNK), CHUNK)
        wq = []
        for h in range(nh):
            hp, i = divmod(h, 2)
            rs = slice(i * CHUNK, (i + 1) * CHUNK)
            lhs = jnp.concatenate([w_st[c * npair + hp, rs, :], qd_st[c * npair + hp, rs, :]], axis=0)
            wq.append(_mm(lhs, state[h]))
        vn = []
        for hp in range(npair):
            w0 = wq[2 * hp]
            w1 = wq[2 * hp + 1]
            vn.append(u_st[c * npair + hp] - jnp.concatenate([w0[0:CHUNK], w1[0:CHUNK]], axis=0))
        oi = [_mm(in_st[c * npair + hp], vn[hp]) for hp in range(npair)]
        for hp in range(npair):
            idx = c * npair + hp
            ket = ket_st[idx]
            for i in range(2):
                h = 2 * hp + i
                rs = slice(i * CHUNK, (i + 1) * CHUNK)
                keep = row_first if i == 0 else jnp.logical_not(row_first)
                state[h] = state[h] * cd_st[idx, i * CHUNK:i * CHUNK + 1, :] + _mm(ket, jnp.where(keep, vn[hp], 0.0))
                o_h = wq[h][CHUNK:c2] + oi[hp][rs]
                o_h = o_h * lax.rsqrt(jnp.mean(o_h * o_h, axis=-1, keepdims=True) + NORM_EPS) * nw_ref[...]
                o_ref[rows, h * hd:(h + 1) * hd] = o_h
        return carry

    lax.fori_loop(0, nc, chunk_body, 0)


def _gdn(qkv, ba, prm, bsz, seq, lb, inv_passes):
    bl, w3 = qkv.shape
    mixw = w3 // 3
    nh = mixw // GDN_HEAD
    per_b = seq // lb
    n = (lb // CHUNK) * (nh // 2)
    c2 = 2 * CHUNK
    full = lambda a: pl.BlockSpec(a.shape, lambda i, t: (0, 0))
    kern = functools.partial(_gdn_kernel, lb=lb, nh=nh, inv_passes=inv_passes)
    vm = lambda *s: pltpu.VMEM(s, F32)
    names = ['cw', 'alog', 'dtb', 'nw']
    return pl.pallas_call(
        kern,
        grid=(bsz, per_b),
        in_specs=[pl.BlockSpec((lb, w3), lambda i, t: (i * per_b + t, 0)),
                  pl.BlockSpec((lb, LANES), lambda i, t: (i * per_b + t, 0))] + [full(prm[k]) for k in names],
        out_specs=pl.BlockSpec((lb, mixw), lambda i, t: (i * per_b + t, 0)),
        out_shape=jax.ShapeDtypeStruct((bl, mixw), F32),
        scratch_shapes=[vm(lb + SUBLANES, w3), vm(nh, GDN_HEAD, GDN_HEAD)]
                       + [vm(n, c2, GDN_HEAD)] * 6
                       + [vm(n, GDN_HEAD, c2), vm(n, c2, c2), vm(n, c2, GDN_HEAD)],
        compiler_params=_cparams(("arbitrary", "arbitrary")),
        name="gdn",
    )(qkv, ba, *[prm[k] for k in names])


def _finish(x_ref, g_ref, acc, fw_ref, o_ref, final):
    xn = x_ref[...] + g_ref[0] * acc
    if final:
        ms = jnp.mean(xn * xn, axis=-1, keepdims=True)
        xn = xn * lax.rsqrt(ms + NORM_EPS) * fw_ref[...]
    o_ref[...] = xn


def _outproj_even_kernel(conv_ref, u_ref, yb_ref, z_ref, x_ref, g_ref, d_ref, gw_ref, gb_ref, w_ref, fw_ref,
                         o_ref, cs_ref, us_ref, *, final):
    nj = conv_ref.shape[0]
    nrow = conv_ref.shape[2]
    for j in range(nj):
        for t in range(S5_T):
            cs_ref[j, pl.ds(t, nrow, stride=S5_T), :] = conv_ref[j, 0, :, t * LANES:(t + 1) * LANES]
            us_ref[j, pl.ds(t, nrow, stride=S5_T), :] = u_ref[j, 0, :, t * LANES:(t + 1) * LANES]
    ya = jnp.concatenate([cs_ref[j] for j in range(nj)], axis=1)
    uu = jnp.concatenate([us_ref[j] for j in range(nj)], axis=1)
    ya = _gelu_tanh(ya + d_ref[...] * uu)
    ya = ya * _sigmoid(_mm(ya, gw_ref[...]) + gb_ref[...])
    wa = ya.shape[1]
    sz = _silu(z_ref[...])
    acc = _mm(ya * sz[:, :wa], w_ref[0:wa, :]) + _mm(yb_ref[...] * sz[:, wa:], w_ref[wa:, :])
    _finish(x_ref, g_ref, acc, fw_ref, o_ref, final)


def _outproj_odd_kernel(y_ref, z_ref, x_ref, g_ref, w_ref, fw_ref, o_ref, *, final):
    acc = _mm(y_ref[...] * _silu(z_ref[...]), w_ref[...])
    _finish(x_ref, g_ref, acc, fw_ref, o_ref, final)


def kernel(x, c, norm_w, ada_w, ada_b, w_out, final_norm_w, even_w_in, s5_lambda_re, s5_lambda_im, s5_log_step, s5_b_re, s5_b_im, s5_c_re, s5_c_im, s5_d, s5_glu_w, s5_glu_b, rwkv_mu, rwkv_w0, rwkv_w_up, rwkv_a0, rwkv_a_up, rwkv_g_up, rwkv_k_k, rwkv_k_a, rwkv_r_k, rwkv_ln_w, rwkv_ln_b, odd_w_in, gdn_conv_w, gdn_a_log, gdn_dt_bias, gdn_norm_w):
    bsz, seq, d = x.shape
    depth = norm_w.shape[0]
    bl = bsz * seq
    tm = min(512, seq)
    lb = min(256, seq)
    inv_passes = 1
    s5w = s5_d.shape[1]
    nj = s5w // LANES
    rw = rwkv_w0.shape[1]
    mixw = w_out.shape[1]
    nh_gdn = mixw // GDN_HEAD
    per_b = seq // tm

    mod = _modulation(c, ada_w, ada_b)
    x2 = x.reshape(bl, d)
    row1 = lambda a: a.reshape(1, -1)
    tile_spec = lambda w: pl.BlockSpec((tm, w), lambda i: (i, 0))
    res_spec = lambda a: pl.BlockSpec(a.shape, lambda i: tuple(0 for _ in a.shape))
    gate_spec = pl.BlockSpec((1, 1, d), lambda i: (i // per_b, 0, 0))
    fw = row1(final_norm_w)

    for layer in range(depth):
        shift = mod[layer, :, 0:d].reshape(bsz, 1, d)
        scale = mod[layer, :, d:2 * d].reshape(bsz, 1, d)
        gate = mod[layer, :, 2 * d:3 * d].reshape(bsz, 1, d)
        i = layer // 2
        final = layer == depth - 1
        nw = row1(norm_w[layer])
        wo = w_out[layer].astype(BF16)
        if layer % 2 == 0:
            fwid = even_w_in.shape[2] - s5w - mixw
            s5_spec = pl.BlockSpec((nj, 1, tm // S5_T, S5_T * LANES), lambda r: (0, r // per_b, r % per_b, 0))
            s5_scratch = pltpu.VMEM((nj, tm, LANES), F32)
            u5, feats, z = _inproj(
                x2, nw, shift, scale, even_w_in[i].astype(BF16),
                functools.partial(_inproj_even_kernel, tn=512),
                (jax.ShapeDtypeStruct((nj, bsz, seq // S5_T, S5_T * LANES), F32),
                 jax.ShapeDtypeStruct((bl, fwid), F32), jax.ShapeDtypeStruct((bl, mixw), F32)),
                (s5_spec, tile_spec(fwid), tile_spec(mixw)),
                seq, tm, scratch=(s5_scratch,))
            conv5 = _s5_scan(u5, _s5_weights(s5_lambda_re[i], s5_lambda_im[i], s5_log_step[i], s5_b_re[i],
                                             s5_b_im[i], s5_c_re[i], s5_c_im[i]))
            zpad = jnp.zeros((LANES - DECAY_LORA, rw), F32)
            ones_bd = jnp.kron(jnp.eye(rw // RWKV_HEAD, dtype=F32), jnp.ones((RWKV_HEAD, RWKV_HEAD), F32))
            prm = dict(mu=row1(rwkv_mu[i]), w0=row1(rwkv_w0[i]),
                       wup=jnp.concatenate([rwkv_w_up[i], zpad], axis=0).astype(BF16),
                       a0=row1(rwkv_a0[i]),
                       aup=jnp.concatenate([zpad, rwkv_a_up[i]], axis=0).astype(BF16),
                       gup=rwkv_g_up[i].astype(BF16), kk=row1(rwkv_k_k[i]), ka=row1(rwkv_k_a[i]),
                       rk=row1(rwkv_r_k[i]), lnw=row1(rwkv_ln_w[i]), lnb=row1(rwkv_ln_b[i]),
                       ones=ones_bd.astype(BF16))
            yb = _rwkv(feats, prm, bsz, seq, lb, inv_passes)
            ins = (conv5, u5, yb, z, x2, gate, row1(s5_d[i]), s5_glu_w[i].astype(BF16), row1(s5_glu_b[i]), wo, fw)
            specs = [s5_spec, s5_spec,
                     tile_spec(rw), tile_spec(mixw), tile_spec(d), gate_spec] + [res_spec(a) for a in ins[6:]]
            kern = functools.partial(_outproj_even_kernel, final=final)
            out_scratch = [s5_scratch, s5_scratch]
        else:
            w_in = odd_w_in[i]
            q_end = 3 * mixw
            ba_w = jnp.concatenate([w_in[:, q_end:q_end + 2 * nh_gdn],
                                    jnp.zeros((d, LANES - 2 * nh_gdn), F32)], axis=1)
            w_r = jnp.concatenate([w_in[:, :q_end], w_in[:, q_end + 2 * nh_gdn:], ba_w], axis=1).astype(BF16)
            qkv, z, ba = _inproj(
                x2, nw, shift, scale, w_r,
                functools.partial(_inproj_odd_kernel, tn=512),
                (jax.ShapeDtypeStruct((bl, q_end), F32), jax.ShapeDtypeStruct((bl, mixw), F32),
                 jax.ShapeDtypeStruct((bl, LANES), F32)),
                (tile_spec(q_end), tile_spec(mixw), tile_spec(LANES)),
                seq, tm)
            pad_row = lambda a: jnp.zeros((1, LANES), F32).at[0, nh_gdn:2 * nh_gdn].set(a)
            prm = dict(cw=gdn_conv_w[i], alog=pad_row(gdn_a_log[i]), dtb=pad_row(gdn_dt_bias[i]),
                       nw=row1(gdn_norm_w[i]))
            y = _gdn(qkv, ba, prm, bsz, seq, lb, inv_passes)
            ins = (y, z, x2, gate, wo, fw)
            specs = [tile_spec(mixw), tile_spec(mixw), tile_spec(d), gate_spec] + [res_spec(a) for a in ins[4:]]
            kern = functools.partial(_outproj_odd_kernel, final=final)
            out_scratch = []
        x2 = pl.pallas_call(
            kern,
            grid=(bl // tm,),
            in_specs=specs,
            out_specs=tile_spec(d),
            out_shape=jax.ShapeDtypeStruct((bl, d), F32),
            scratch_shapes=out_scratch,
            compiler_params=_cparams(("arbitrary",)),
            name="out_proj",
        )(*ins)
    return x2.reshape(bsz, seq, d)
```

```python
import functools
import math

import jax
import jax.numpy as jnp
from jax import lax
from jax.experimental import pallas as pl
from jax.experimental.pallas import tpu as pltpu

F32 = jnp.float32
BF16 = jnp.bfloat16

NORM_EPS = 1e-6
LANES = 128
SUBLANES = 8
VMEM_LIMIT = 56 * 1024 * 1024

S5_GROUP = 16
S5_STATE = 64
RWKV_HEAD = 64
DECAY_LORA = 64
ICLR_LORA = 64
GATE_LORA = 128
RWKV_DECAY_SCALE = math.exp(-0.5)
GDN_HEAD = 128
GDN_CONV = 4
CHUNK = 64
S5_T = 8

NN = (((1,), (0,)), ((), ()))
NT = (((1,), (1,)), ((), ()))
BNN = (((2,), (1,)), ((0,), (0,)))
BNT = (((2,), (2,)), ((0,), (0,)))


def _split(a, n):
    if a.dtype == BF16:
        return [a]
    parts = []
    r = a
    for i in range(n):
        p = r.astype(BF16)
        parts.append(p)
        if i + 1 < n:
            r = r - p.astype(F32)
    return parts


def _mm(a, b, dn=NN, pa=1, pb=1):
    ap = _split(a, pa)
    bp = _split(b, pb)
    lim = max(len(ap), len(bp))
    acc = None
    for i, x in enumerate(ap):
        for j, y in enumerate(bp):
            if i + j >= lim:
                continue
            t = lax.dot_general(x, y, dn, preferred_element_type=F32)
            acc = t if acc is None else acc + t
    return acc


def _sigmoid(x):
    return 0.5 + 0.5 * jnp.tanh(0.5 * x)


def _silu(x):
    return x * _sigmoid(x)


def _softplus(x):
    return jnp.maximum(x, 0.0) + jnp.log(1.0 + jnp.exp(-jnp.abs(x)))


def _gelu_tanh(x):
    c = math.sqrt(2.0 / math.pi)
    return 0.5 * x * (1.0 + jnp.tanh(c * (x + 0.044715 * (x * x * x))))


def _iota2(shape, dim):
    return lax.broadcasted_iota(jnp.int32, shape, dim)


def _inv_unit_lower(n_mat, eye, passes, dn):
    size = n_mat.shape[-1]
    row = _iota2((size, size), 0)
    col = _iota2((size, size), 1)
    t = eye + jnp.where((row // 2) == (col // 2), n_mat, 0.0)
    b = 2
    while b < CHUNK:
        off = ((row // (2 * b)) == (col // (2 * b))) & ((row // b) % 2 == 1) & ((col // b) % 2 == 0)
        x = _mm(jnp.where(off, n_mat, 0.0), t, dn, pa=passes, pb=passes)
        t = t + _mm(t, x, dn, pa=passes, pb=passes)
        b *= 2
    return t


def _cparams(sem):
    return pltpu.CompilerParams(dimension_semantics=sem, vmem_limit_bytes=VMEM_LIMIT)


def _mod_kernel(c_ref, w_ref, b_ref, o_ref):
    s = _silu(c_ref[...])
    o_ref[0] = _mm(s, w_ref[0], pa=2, pb=2) + b_ref[0]


def _modulation(c, ada_w, ada_b):
    depth, d, d3 = ada_w.shape
    bsz = c.shape[0]
    nj = d3 // d
    return pl.pallas_call(
        _mod_kernel,
        grid=(depth, nj),
        in_specs=[
            pl.BlockSpec((bsz, d), lambda l, j: (0, 0)),
            pl.BlockSpec((1, d, d), lambda l, j: (l, 0, j)),
            pl.BlockSpec((1, 1, d), lambda l, j: (l, 0, j)),
        ],
        out_specs=pl.BlockSpec((1, bsz, d), lambda l, j: (l, 0, j)),
        out_shape=jax.ShapeDtypeStruct((depth, bsz, d3), F32),
        compiler_params=_cparams(("arbitrary", "arbitrary")),
        name="adaln_mod",
    )(c, ada_w, ada_b.reshape(depth, 1, d3))


def _norm_mod(x, nw, scale, shift):
    ms = jnp.mean(x * x, axis=-1, keepdims=True)
    h = x * lax.rsqrt(ms + NORM_EPS) * nw
    return h * (1.0 + scale) + shift


def _inproj_even_kernel(x_ref, nw_ref, sh_ref, sc_ref, w_ref, u_ref, f_ref, z_ref, us_ref, *, tn):
    hb = _norm_mod(x_ref[...], nw_ref[...], sc_ref[0], sh_ref[0]).astype(BF16)
    nu = u_ref.shape[0]
    nrow = u_ref.shape[2]
    for j in range(nu):
        us_ref[j] = _mm(hb, w_ref[:, j * LANES:(j + 1) * LANES])
        for t in range(S5_T):
            u_ref[j, 0, :, t * LANES:(t + 1) * LANES] = us_ref[j, pl.ds(t, nrow, stride=S5_T), :]
    off = nu * LANES
    for o_ref in (f_ref, z_ref):
        width = o_ref.shape[1]
        for j0 in range(0, width, tn):
            w = min(tn, width - j0)
            o_ref[:, j0:j0 + w] = _mm(hb, w_ref[:, off + j0:off + j0 + w]).astype(o_ref.dtype)
        off += width


def _inproj_odd_kernel(x_ref, nw_ref, sh_ref, sc_ref, w_ref, qkv_ref, z_ref, ba_ref, *, tn):
    hb = _norm_mod(x_ref[...], nw_ref[...], sc_ref[0], sh_ref[0]).astype(BF16)
    off = 0
    for o_ref in (qkv_ref, z_ref, ba_ref):
        width = o_ref.shape[1]
        for j0 in range(0, width, tn):
            w = min(tn, width - j0)
            o_ref[:, j0:j0 + w] = _mm(hb, w_ref[:, off + j0:off + j0 + w]).astype(o_ref.dtype)
        off += width


def _inproj(x2, nw, shift, scale, w_bf, kernel_fn, out_shapes, out_specs, seq, tm, scratch=(), extra=()):
    bl, d = x2.shape
    n = w_bf.shape[1]
    per_b = seq // tm
    return pl.pallas_call(
        kernel_fn,
        grid=(bl // tm,),
        in_specs=[
            pl.BlockSpec((tm, d), lambda i: (i, 0)),
            pl.BlockSpec((1, d), lambda i: (0, 0)),
            pl.BlockSpec((1, 1, d), lambda i: (i // per_b, 0, 0)),
            pl.BlockSpec((1, 1, d), lambda i: (i // per_b, 0, 0)),
            pl.BlockSpec((d, n), lambda i: (0, 0)),
        ] + [pl.BlockSpec(a.shape, lambda i: (0, 0)) for a in extra],
        out_specs=out_specs,
        out_shape=out_shapes,
        scratch_shapes=list(scratch),
        compiler_params=_cparams(("arbitrary",)),
        name="in_proj",
    )(x2, nw, shift, scale, w_bf, *extra)


def _s5_kernel(u_ref, kd_ref, p_ref, q_ref, are_ref, aim_ref, y_ref, st_ref, sloc_ref, sprev_ref, m_ref, *, tc, nb):
    cb = pl.program_id(1)

    @pl.when(cb == 0)
    def _():
        st_ref[...] = jnp.zeros_like(st_ref)
        for s in range(S5_T):
            for t in range(S5_T):
                blk = kd_ref[t - s, 0].astype(BF16) if t >= s else jnp.zeros((LANES, LANES), BF16)
                m_ref[s * LANES:(s + 1) * LANES, t * LANES:(t + 1) * LANES] = blk

    kcols = u_ref.shape[-1]
    x = u_ref[0].reshape(nb * tc, kcols).astype(BF16)
    y = _mm(x, m_ref[...])
    sloc = _mm(x, p_ref[0])
    nt2 = sloc_ref.shape[0]
    nt = nt2 // 2
    pitch = tc + SUBLANES
    for k in range(nt2):
        for b in range(nb):
            sloc_ref[k, b * pitch:b * pitch + tc, :] = sloc[b * tc:(b + 1) * tc, k * LANES:(k + 1) * LANES]
    a_re = are_ref[0]
    a_im = aim_ref[0]

    def body(c, carry):
        rows = pl.ds(c, nb, stride=pitch)
        new = []
        for k in range(nt):
            s_re, s_im = carry[k], carry[nt + k]
            sprev_ref[k, rows, :] = s_re
            sprev_ref[nt + k, rows, :] = s_im
            ar = a_re[:, k * LANES:(k + 1) * LANES]
            ai = a_im[:, k * LANES:(k + 1) * LANES]
            new.append((ar * s_re - ai * s_im + sloc_ref[k, rows, :],
                        ar * s_im + ai * s_re + sloc_ref[nt + k, rows, :]))
        return tuple(n[0] for n in new) + tuple(n[1] for n in new)

    st = lax.fori_loop(0, tc, body, tuple(st_ref[k] for k in range(nt2)))
    for k in range(nt2):
        st_ref[k] = st[k]
    sprev = jnp.concatenate(
        [jnp.concatenate([sprev_ref[k, b * pitch:b * pitch + tc, :] for b in range(nb)], axis=0)
         for k in range(nt2)], axis=1)
    y = y + _mm(sprev, q_ref[0])
    y_ref[0] = y.reshape(nb, tc, kcols)


def _s5_weights(lambda_re, lambda_im, log_step, b_re, b_im, c_re, c_im):
    g_all, p_all = lambda_re.shape
    t_len = S5_T
    gpt = LANES // S5_GROUP
    nj = g_all // gpt
    lam = lax.complex(jnp.minimum(lambda_re, -1e-4), lambda_im)
    step = jnp.exp(log_step)[:, None]
    lam_bar = jnp.exp(lam * step)
    b_bar = ((lam_bar - 1.0) / lam)[..., None] * lax.complex(b_re, b_im)
    c_mat = lax.complex(c_re, c_im)
    taus = jnp.arange(t_len + 1, dtype=F32)
    pw = jnp.exp((lam * step)[None] * taus[:, None, None])
    hi = lax.Precision.HIGHEST
    kern = jnp.real(jnp.einsum('gcp,tgp,gpd->tgcd', c_mat, pw[:t_len], b_bar, precision=hi))
    s_idx = jnp.arange(t_len)
    eye_g = jnp.eye(gpt, dtype=F32)
    kd = jnp.einsum('tjgoi,hg->tjhigo', kern.reshape(t_len, nj, gpt, S5_GROUP, S5_GROUP), eye_g)
    kd = kd.reshape(t_len, nj, LANES, LANES)
    pb = pw[t_len - 1 - s_idx][:, :, :, None] * b_bar[None]
    pb = jnp.stack([jnp.real(pb), jnp.imag(pb)], axis=0)
    pb = pb.reshape(2, t_len, nj, gpt, p_all, S5_GROUP)
    p_mat = jnp.einsum('rsjgpi,hg->jshirgp', pb, eye_g).reshape(nj, t_len * LANES, 2 * gpt * p_all)
    cq = c_mat[None] * pw[1:t_len + 1][:, :, None, :]
    cq = jnp.stack([jnp.real(cq), -jnp.imag(cq)], axis=0)
    cq = cq.reshape(2, t_len, nj, gpt, S5_GROUP, p_all)
    q_mat = jnp.einsum('rtjgop,hg->jrhptgo', cq, eye_g).reshape(nj, 2 * gpt * p_all, t_len * LANES)
    a_t = pw[t_len].reshape(nj, 1, gpt * p_all)
    return (kd, p_mat.astype(BF16), q_mat.astype(BF16),
            jnp.real(a_t).astype(F32), jnp.imag(a_t).astype(F32))


def _s5_scan(u5, weights):
    kd, p_mat, q_mat, a_re, a_im = weights
    nj, bsz, nchunks, kcols = u5.shape
    tc = min(64, nchunks)
    ns2 = p_mat.shape[-1]
    kern = functools.partial(_s5_kernel, tc=tc, nb=bsz)
    return pl.pallas_call(
        kern,
        grid=(nj, nchunks // tc),
        in_specs=[
            pl.BlockSpec((1, bsz, tc, kcols), lambda j, c: (j, 0, c, 0)),
            pl.BlockSpec((S5_T, 1, LANES, LANES), lambda j, c: (0, j, 0, 0)),
            pl.BlockSpec((1, kcols, ns2), lambda j, c: (j, 0, 0)),
            pl.BlockSpec((1, ns2, kcols), lambda j, c: (j, 0, 0)),
            pl.BlockSpec((1, 1, ns2 // 2), lambda j, c: (j, 0, 0)),
            pl.BlockSpec((1, 1, ns2 // 2), lambda j, c: (j, 0, 0)),
        ],
        out_specs=pl.BlockSpec((1, bsz, tc, kcols), lambda j, c: (j, 0, c, 0)),
        out_shape=jax.ShapeDtypeStruct((nj, bsz, nchunks, kcols), F32),
        scratch_shapes=[
            pltpu.VMEM((ns2 // LANES, bsz, LANES), F32),
            pltpu.VMEM((ns2 // LANES, bsz * (tc + SUBLANES), LANES), F32),
            pltpu.VMEM((ns2 // LANES, bsz * (tc + SUBLANES), LANES), F32),
            pltpu.VMEM((kcols, kcols), BF16),
        ],
        compiler_params=_cparams(("arbitrary", "arbitrary")),
        name="s5_scan",
    )(u5, kd, p_mat, q_mat, a_re, a_im)


def _rwkv_kernel(f_ref, mu_ref, w0_ref, wup_ref, a0_ref, aup_ref, gup_ref, kk_ref, ka_ref, rk_ref,
                 lnw_ref, lnb_ref, ones_ref, o_ref,
                 fbuf, state, a_st, r_st, b_st, k_st, v_st, arw_st, tav_st, rkv_st, arb_st, bkd_st, vk_st,
                 ee_st, o_s, *, lb, inv_passes):
    t = pl.program_id(1)
    width = o_s.shape[1]
    npair = width // LANES
    nc = lb // CHUNK
    c2 = 2 * CHUNK

    @pl.when(t == 0)
    def _():
        fbuf[0:SUBLANES, :] = jnp.zeros((SUBLANES, fbuf.shape[1]), F32)
        state[...] = jnp.zeros_like(state)

    @pl.when(t > 0)
    def _():
        fbuf[0:SUBLANES, :] = fbuf[lb:lb + SUBLANES, :]

    x = f_ref[...]
    fbuf[SUBLANES:SUBLANES + lb, :] = x
    xs = fbuf[pl.ds(SUBLANES - 1, lb), :]
    f = x + mu_ref[...] * (xs - x)
    r = f[:, 0:width]
    k = f[:, width:2 * width]
    v = f[:, 2 * width:3 * width]
    lora = f[:, 3 * width:3 * width + LANES]
    xg = f[:, 3 * width + LANES:3 * width + 2 * LANES]
    lw = -RWKV_DECAY_SCALE * _sigmoid(w0_ref[...] + _mm(jnp.tanh(lora), wup_ref[...]))
    a = _sigmoid(a0_ref[...] + _mm(lora, aup_ref[...]))
    g = _mm(_sigmoid(xg), gup_ref[...])
    ones_bd = ones_ref[...]
    kk = k * kk_ref[...]
    kn = kk * lax.rsqrt(_mm(kk * kk, ones_bd) + 1e-6)
    kp = k * (1.0 + (a - 1.0) * ka_ref[...])
    bonus = _mm(r * kp * rk_ref[...], ones_bd) * v
    bb = kn * a

    row = _iota2((c2, c2), 0)
    col = _iota2((c2, c2), 1)
    same = (row // CHUNK) == (col // CHUNK)
    strict = same & (row > col)
    incl = same & (row >= col)
    eye = jnp.where(row == col, 1.0, 0.0).astype(F32)
    tri = jnp.where(_iota2((CHUNK, CHUNK), 0) >= _iota2((CHUNK, CHUNK), 1), 1.0, 0.0).astype(BF16)
    first = _iota2((CHUNK, LANES), 1) < RWKV_HEAD

    def stack(m):
        return jnp.concatenate([jnp.where(first, m, 0.0), jnp.where(first, 0.0, m)], axis=0)

    for c in range(nc):
        cs = slice(c * CHUNK, (c + 1) * CHUNK)
        lwc = lw[cs]
        gc = _mm(tri, lwc, pb=3)
        e_g = jnp.exp(gc)
        e_ng = jnp.exp(-gc)
        at = -kn[cs] * jnp.exp(gc - lwc)
        bt = bb[cs] * e_ng
        kt = kp[cs] * e_ng
        rt = r[cs] * e_g
        vv = v[cs]
        e_end = jnp.broadcast_to(jnp.exp(gc[CHUNK - 1:CHUNK, :]), (SUBLANES, width))
        for p in range(npair):
            idx = c * npair + p
            sl = slice(p * LANES, (p + 1) * LANES)
            a_st[idx] = stack(at[:, sl])
            r_st[idx] = stack(rt[:, sl])
            b_st[idx] = stack(bt[:, sl])
            k_st[idx] = stack(kt[:, sl])
            v_st[idx] = stack(vv[:, sl])
            ee_st[idx] = e_end[:, sl]

    a_all = a_st[...]
    r_all = r_st[...]
    v_all = v_st[...]
    ar = jnp.concatenate([a_all, r_all], axis=1)
    bk = jnp.concatenate([b_st[...], k_st[...]], axis=1)
    p4 = _mm(ar, bk, BNT)
    a_ab = jnp.where(strict, p4[:, 0:c2, 0:c2], 0.0)
    a_ak = jnp.where(strict, p4[:, 0:c2, c2:2 * c2], 0.0)
    a_rb = jnp.where(incl, p4[:, c2:2 * c2, 0:c2], 0.0)
    a_rk = jnp.where(incl, p4[:, c2:2 * c2, c2:2 * c2], 0.0)
    t_inv = _inv_unit_lower(a_ab, eye, inv_passes, BNN)
    arw_st[...] = jnp.concatenate([_mm(t_inv, a_all, BNN, pa=inv_passes), r_all], axis=1)
    tav_st[...] = _mm(t_inv, _mm(a_ak, v_all, BNN), BNN, pa=inv_passes)
    rkv_st[...] = _mm(a_rk, v_all, BNN)
    arb_st[...] = a_rb
    bkd = bk * ee_st[:, 0:1, :]
    bkd_st[...] = bkd[:, 0:c2, :]
    vk_st[...] = _mm(jnp.swapaxes(v_all, 1, 2), bkd[:, c2:2 * c2, :], BNN)

    def chunk_body(c, carry):
        rows = pl.ds(pl.multiple_of(c * CHUNK, CHUNK), CHUNK)
        xr = [_mm(arw_st[c * npair + p], state[p], NT) for p in range(npair)]
        uu = [xr[p][0:c2] + tav_st[c * npair + p] for p in range(npair)]
        ob = [_mm(arb_st[c * npair + p], uu[p]) for p in range(npair)]
        for p in range(npair):
            idx = c * npair + p
            o_st = xr[p][c2:2 * c2] + ob[p] + rkv_st[idx]
            o_s[rows, p * LANES:(p + 1) * LANES] = o_st[0:CHUNK] + o_st[CHUNK:c2]
            state[p] = state[p] * ee_st[idx, 0:1, :] + vk_st[idx] + _mm(uu[p].T, bkd_st[idx])
        return carry

    lax.fori_loop(0, nc, chunk_body, 0)

    o = o_s[...]
    inv_n = 1.0 / RWKV_HEAD
    mean = _mm(o, ones_bd) * inv_n
    d = o - mean
    var = _mm(d * d, ones_bd) * inv_n
    on = d * lax.rsqrt(var + 1e-5 * RWKV_HEAD) * lnw_ref[...] + lnb_ref[...]
    o_ref[...] = ((on + bonus) * g).astype(o_ref.dtype)


def _rwkv(feats, prm, bsz, seq, lb, inv_passes):
    bl, fw = feats.shape
    width = prm['w0'].shape[1]
    full = lambda a: pl.BlockSpec(a.shape, lambda i, t: (0, 0))
    names = ['mu', 'w0', 'wup', 'a0', 'aup', 'gup', 'kk', 'ka', 'rk', 'lnw', 'lnb', 'ones']
    per_b = seq // lb
    npair = width // LANES
    n = (lb // CHUNK) * npair
    c2 = 2 * CHUNK
    kern = functools.partial(_rwkv_kernel, lb=lb, inv_passes=inv_passes)
    vm = lambda *s: pltpu.VMEM(s, F32)
    return pl.pallas_call(
        kern,
        grid=(bsz, per_b),
        in_specs=[pl.BlockSpec((lb, fw), lambda i, t: (i * per_b + t, 0))] + [full(prm[k]) for k in names],
        out_specs=pl.BlockSpec((lb, width), lambda i, t: (i * per_b + t, 0)),
        out_shape=jax.ShapeDtypeStruct((bl, width), BF16),
        scratch_shapes=[vm(lb + SUBLANES, fw), vm(npair, LANES, LANES)]
                       + [vm(n, c2, LANES)] * 5
                       + [vm(n, 2 * c2, LANES), vm(n, c2, LANES), vm(n, c2, LANES), vm(n, c2, c2),
                          vm(n, c2, LANES), vm(n, LANES, LANES), vm(n, SUBLANES, LANES), vm(lb, width)],
        compiler_params=_cparams(("arbitrary", "arbitrary")),
        name="rwkv7",
    )(feats, *[prm[k] for k in names])


def _gdn_kernel(qkv_ref, ba_ref, cw_ref, alog_ref, dtb_ref, nw_ref, o_ref,
                cbuf, state, q_st, k_st, v_st, u_st, w_st, qd_st, ket_st, in_st, cd_st, *, lb, nh, inv_passes):
    t = pl.program_id(1)
    hd = GDN_HEAD
    nc = lb // CHUNK
    npair = nh // 2
    c2 = 2 * CHUNK
    mixw = nh * hd

    @pl.when(t == 0)
    def _():
        cbuf[0:SUBLANES, :] = jnp.zeros((SUBLANES, cbuf.shape[1]), F32)
        state[...] = jnp.zeros_like(state)

    @pl.when(t > 0)
    def _():
        cbuf[0:SUBLANES, :] = cbuf[lb:lb + SUBLANES, :]

    cbuf[SUBLANES:SUBLANES + lb, :] = qkv_ref[...]

    def conv_silu(off):
        acc = None
        for j in range(GDN_CONV):
            term = cw_ref[j:j + 1, off:off + hd] * cbuf[pl.ds(SUBLANES - GDN_CONV + 1 + j, lb), off:off + hd]
            acc = term if acc is None else acc + term
        return _silu(acc)

    for h in range(nh):
        q = conv_silu(h * hd)
        k = conv_silu(mixw + h * hd)
        v = conv_silu(2 * mixw + h * hd)
        q = q * lax.rsqrt(jnp.sum(q * q, axis=-1, keepdims=True) + 1e-6) * (hd ** -0.5)
        k = k * lax.rsqrt(jnp.sum(k * k, axis=-1, keepdims=True) + 1e-6)
        hp, i = divmod(h, 2)
        rs = slice(i * CHUNK, (i + 1) * CHUNK)
        for c in range(nc):
            cs = slice(c * CHUNK, (c + 1) * CHUNK)
            q_st[c * npair + hp, rs, :] = q[cs]
            k_st[c * npair + hp, rs, :] = k[cs]
            v_st[c * npair + hp, rs, :] = v[cs]

    ba = ba_ref[...]
    beta = _sigmoid(ba)
    ld = -jnp.exp(alog_ref[...]) * _softplus(ba + dtb_ref[...])

    row = _iota2((c2, c2), 0)
    col = _iota2((c2, c2), 1)
    same = (row // CHUNK) == (col // CHUNK)
    strict = same & (row > col)
    incl = same & (row >= col)
    eye = jnp.where(row == col, 1.0, 0.0).astype(F32)
    tri = jnp.where(_iota2((CHUNK, CHUNK), 0) >= _iota2((CHUNK, CHUNK), 1), 1.0, 0.0).astype(BF16)
    lane_first = _iota2((1, c2), 1) < CHUNK
    row_first = _iota2((c2, hd), 0) < CHUNK

    gcols, grows, bcols, gends = [], [], [], []
    for c in range(nc):
        cs = slice(c * CHUNK, (c + 1) * CHUNK)
        g = _mm(tri, ld[cs], pb=3)
        g2t = jnp.concatenate([g, g], axis=0).T
        bc = beta[cs]
        for hp in range(npair):
            l0 = nh + 2 * hp
            l1 = l0 + 1
            gcols.append(jnp.concatenate([g[:, l0:l0 + 1], g[:, l1:l1 + 1]], axis=0))
            grows.append(jnp.where(lane_first, g2t[l0:l0 + 1, :], g2t[l1:l1 + 1, :]))
            bcols.append(jnp.concatenate([bc[:, 2 * hp:2 * hp + 1], bc[:, 2 * hp + 1:2 * hp + 2]], axis=0))
            gends.append(jnp.concatenate([jnp.broadcast_to(g[CHUNK - 1:CHUNK, l0:l0 + 1], (CHUNK, 1)),
                                          jnp.broadcast_to(g[CHUNK - 1:CHUNK, l1:l1 + 1], (CHUNK, 1))], axis=0))
    gcol = jnp.stack(gcols)
    grow = jnp.stack(grows)
    bcol = jnp.stack(bcols)
    gend = jnp.stack(gends)

    q_all = q_st[...]
    k_all = k_st[...]
    v_all = v_st[...]
    dec = jnp.where(incl, jnp.exp(jnp.where(incl, gcol - grow, 0.0)), 0.0)
    e_g = jnp.exp(gcol)
    kb = k_all * bcol
    a_mat = jnp.where(strict, _mm(kb, k_all, BNT) * dec, 0.0)
    in_st[...] = jnp.where(incl, _mm(q_all, k_all, BNT) * dec, 0.0)
    t_inv = _inv_unit_lower(-a_mat, eye, inv_passes, BNN)
    uw = _mm(t_inv, jnp.concatenate([v_all * bcol, kb * e_g], axis=2), BNN, pa=inv_passes)
    u_st[...] = uw[:, :, 0:hd]
    w_st[...] = uw[:, :, hd:2 * hd]
    qd_st[...] = q_all * e_g
    ket_st[...] = jnp.swapaxes(k_all * jnp.exp(gend - gcol), 1, 2)
    cd_st[...] = jnp.broadcast_to(jnp.exp(gend), cd_st.shape)

    def chunk_body(c, carry):
        rows = pl.ds(pl.multiple_of(c * CHUNK, CHUNK), CHUNK)
        wq = []
        for h in range(nh):
            hp, i = divmod(h, 2)
            rs = slice(i * CHUNK, (i + 1) * CHUNK)
            lhs = jnp.concatenate([w_st[c * npair + hp, rs, :], qd_st[c * npair + hp, rs, :]], axis=0)
            wq.append(_mm(lhs, state[h]))
        vn = []
        for hp in range(npair):
            w0 = wq[2 * hp]
            w1 = wq[2 * hp + 1]
            vn.append(u_st[c * npair + hp] - jnp.concatenate([w0[0:CHUNK], w1[0:CHUNK]], axis=0))
        oi = [_mm(in_st[c * npair + hp], vn[hp]) for hp in range(npair)]
        for hp in range(npair):
            idx = c * npair + hp
            ket = ket_st[idx]
            for i in range(2):
                h = 2 * hp + i
                rs = slice(i * CHUNK, (i + 1) * CHUNK)
                keep = row_first if i == 0 else jnp.logical_not(row_first)
                state[h] = state[h] * cd_st[idx, i * CHUNK:i * CHUNK + 1, :] + _mm(ket, jnp.where(keep, vn[hp], 0.0))
                o_h = wq[h][CHUNK:c2] + oi[hp][rs]
                o_h = o_h * lax.rsqrt(jnp.mean(o_h * o_h, axis=-1, keepdims=True) + NORM_EPS) * nw_ref[...]
                o_ref[rows, h * hd:(h + 1) * hd] = o_h.astype(o_ref.dtype)
        return carry

    lax.fori_loop(0, nc, chunk_body, 0)


def _gdn(qkv, ba, prm, bsz, seq, lb, inv_passes):
    bl, w3 = qkv.shape
    mixw = w3 // 3
    nh = mixw // GDN_HEAD
    per_b = seq // lb
    n = (lb // CHUNK) * (nh // 2)
    c2 = 2 * CHUNK
    full = lambda a: pl.BlockSpec(a.shape, lambda i, t: (0, 0))
    kern = functools.partial(_gdn_kernel, lb=lb, nh=nh, inv_passes=inv_passes)
    vm = lambda *s: pltpu.VMEM(s, F32)
    names = ['cw', 'alog', 'dtb', 'nw']
    return pl.pallas_call(
        kern,
        grid=(bsz, per_b),
        in_specs=[pl.BlockSpec((lb, w3), lambda i, t: (i * per_b + t, 0)),
                  pl.BlockSpec((lb, LANES), lambda i, t: (i * per_b + t, 0))] + [full(prm[k]) for k in names],
        out_specs=pl.BlockSpec((lb, mixw), lambda i, t: (i * per_b + t, 0)),
        out_shape=jax.ShapeDtypeStruct((bl, mixw), BF16),
        scratch_shapes=[vm(lb + SUBLANES, w3), vm(nh, GDN_HEAD, GDN_HEAD)]
                       + [vm(n, c2, GDN_HEAD)] * 6
                       + [vm(n, GDN_HEAD, c2), vm(n, c2, c2), vm(n, c2, GDN_HEAD)],
        compiler_params=_cparams(("arbitrary", "arbitrary")),
        name="gdn",
    )(qkv, ba, *[prm[k] for k in names])


def _finish(x_ref, g_ref, acc, fw_ref, o_ref, final):
    xn = x_ref[...] + g_ref[0] * acc
    if final:
        ms = jnp.mean(xn * xn, axis=-1, keepdims=True)
        xn = xn * lax.rsqrt(ms + NORM_EPS) * fw_ref[...]
    o_ref[...] = xn


def _outproj_even_kernel(conv_ref, u_ref, yb_ref, z_ref, x_ref, g_ref, d_ref, gw_ref, gb_ref, w_ref, fw_ref,
                         o_ref, cs_ref, us_ref, *, final):
    nj = conv_ref.shape[0]
    nrow = conv_ref.shape[2]
    for j in range(nj):
        for t in range(S5_T):
            cs_ref[j, pl.ds(t, nrow, stride=S5_T), :] = conv_ref[j, 0, :, t * LANES:(t + 1) * LANES]
            us_ref[j, pl.ds(t, nrow, stride=S5_T), :] = u_ref[j, 0, :, t * LANES:(t + 1) * LANES]
    ya = jnp.concatenate([cs_ref[j] for j in range(nj)], axis=1)
    uu = jnp.concatenate([us_ref[j] for j in range(nj)], axis=1)
    ya = _gelu_tanh(ya + d_ref[...] * uu)
    ya = ya * _sigmoid(_mm(ya, gw_ref[...]) + gb_ref[...])
    wa = ya.shape[1]
    sz = _silu(z_ref[...].astype(F32))
    acc = _mm(ya * sz[:, :wa], w_ref[0:wa, :]) + _mm(yb_ref[...].astype(F32) * sz[:, wa:], w_ref[wa:, :])
    _finish(x_ref, g_ref, acc, fw_ref, o_ref, final)


def _outproj_odd_kernel(y_ref, z_ref, x_ref, g_ref, w_ref, fw_ref, o_ref, *, final):
    acc = _mm(y_ref[...].astype(F32) * _silu(z_ref[...].astype(F32)), w_ref[...])
    _finish(x_ref, g_ref, acc, fw_ref, o_ref, final)


def kernel(x, c, norm_w, ada_w, ada_b, w_out, final_norm_w, even_w_in, s5_lambda_re, s5_lambda_im, s5_log_step, s5_b_re, s5_b_im, s5_c_re, s5_c_im, s5_d, s5_glu_w, s5_glu_b, rwkv_mu, rwkv_w0, rwkv_w_up, rwkv_a0, rwkv_a_up, rwkv_g_up, rwkv_k_k, rwkv_k_a, rwkv_r_k, rwkv_ln_w, rwkv_ln_b, odd_w_in, gdn_conv_w, gdn_a_log, gdn_dt_bias, gdn_norm_w):
    bsz, seq, d = x.shape
    depth = norm_w.shape[0]
    bl = bsz * seq
    tm = min(512, seq)
    lb = min(256, seq)
    inv_passes = 1
    s5w = s5_d.shape[1]
    nj = s5w // LANES
    rw = rwkv_w0.shape[1]
    mixw = w_out.shape[1]
    nh_gdn = mixw // GDN_HEAD
    per_b = seq // tm

    mod = _modulation(c, ada_w, ada_b)
    x2 = x.reshape(bl, d)
    row1 = lambda a: a.reshape(1, -1)
    tile_spec = lambda w: pl.BlockSpec((tm, w), lambda i: (i, 0))
    res_spec = lambda a: pl.BlockSpec(a.shape, lambda i: tuple(0 for _ in a.shape))
    gate_spec = pl.BlockSpec((1, 1, d), lambda i: (i // per_b, 0, 0))
    fw = row1(final_norm_w)

    for layer in range(depth):
        shift = mod[layer, :, 0:d].reshape(bsz, 1, d)
        scale = mod[layer, :, d:2 * d].reshape(bsz, 1, d)
        gate = mod[layer, :, 2 * d:3 * d].reshape(bsz, 1, d)
        i = layer // 2
        final = layer == depth - 1
        nw = row1(norm_w[layer])
        wo = w_out[layer].astype(BF16)
        if layer % 2 == 0:
            fwid = even_w_in.shape[2] - s5w - mixw
            s5_spec = pl.BlockSpec((nj, 1, tm // S5_T, S5_T * LANES), lambda r: (0, r // per_b, r % per_b, 0))
            s5_scratch = pltpu.VMEM((nj, tm, LANES), F32)
            u5, feats, z = _inproj(
                x2, nw, shift, scale, even_w_in[i].astype(BF16),
                functools.partial(_inproj_even_kernel, tn=512),
                (jax.ShapeDtypeStruct((nj, bsz, seq // S5_T, S5_T * LANES), F32),
                 jax.ShapeDtypeStruct((bl, fwid), F32), jax.ShapeDtypeStruct((bl, mixw), BF16)),
                (s5_spec, tile_spec(fwid), tile_spec(mixw)),
                seq, tm, scratch=(s5_scratch,))
            conv5 = _s5_scan(u5, _s5_weights(s5_lambda_re[i], s5_lambda_im[i], s5_log_step[i], s5_b_re[i],
                                             s5_b_im[i], s5_c_re[i], s5_c_im[i]))
            zpad = jnp.zeros((LANES - DECAY_LORA, rw), F32)
            ones_bd = jnp.kron(jnp.eye(rw // RWKV_HEAD, dtype=F32), jnp.ones((RWKV_HEAD, RWKV_HEAD), F32))
            prm = dict(mu=row1(rwkv_mu[i]), w0=row1(rwkv_w0[i]),
                       wup=jnp.concatenate([rwkv_w_up[i], zpad], axis=0).astype(BF16),
                       a0=row1(rwkv_a0[i]),
                       aup=jnp.concatenate([zpad, rwkv_a_up[i]], axis=0).astype(BF16),
                       gup=rwkv_g_up[i].astype(BF16), kk=row1(rwkv_k_k[i]), ka=row1(rwkv_k_a[i]),
                       rk=row1(rwkv_r_k[i]), lnw=row1(rwkv_ln_w[i]), lnb=row1(rwkv_ln_b[i]),
                       ones=ones_bd.astype(BF16))
            yb = _rwkv(feats, prm, bsz, seq, lb, inv_passes)
            ins = (conv5, u5, yb, z, x2, gate, row1(s5_d[i]), s5_glu_w[i].astype(BF16), row1(s5_glu_b[i]), wo, fw)
            specs = [s5_spec, s5_spec,
                     tile_spec(rw), tile_spec(mixw), tile_spec(d), gate_spec] + [res_spec(a) for a in ins[6:]]
            kern = functools.partial(_outproj_even_kernel, final=final)
            out_scratch = [s5_scratch, s5_scratch]
        else:
            w_in = odd_w_in[i]
            q_end = 3 * mixw
            ba_w = jnp.concatenate([w_in[:, q_end:q_end + 2 * nh_gdn],
                                    jnp.zeros((d, LANES - 2 * nh_gdn), F32)], axis=1)
            w_r = jnp.concatenate([w_in[:, :q_end], w_in[:, q_end + 2 * nh_gdn:], ba_w], axis=1).astype(BF16)
            qkv, z, ba = _inproj(
                x2, nw, shift, scale, w_r,
                functools.partial(_inproj_odd_kernel, tn=512),
                (jax.ShapeDtypeStruct((bl, q_end), F32), jax.ShapeDtypeStruct((bl, mixw), BF16),
                 jax.ShapeDtypeStruct((bl, LANES), F32)),
                (tile_spec(q_end), tile_spec(mixw), tile_spec(LANES)),
                seq, tm)
            pad_row = lambda a: jnp.zeros((1, LANES), F32).at[0, nh_gdn:2 * nh_gdn].set(a)
            prm = dict(cw=gdn_conv_w[i], alog=pad_row(gdn_a_log[i]), dtb=pad_row(gdn_dt_bias[i]),
                       nw=row1(gdn_norm_w[i]))
            y = _gdn(qkv, ba, prm, bsz, seq, lb, inv_passes)
            ins = (y, z, x2, gate, wo, fw)
            specs = [tile_spec(mixw), tile_spec(mixw), tile_spec(d), gate_spec] + [res_spec(a) for a in ins[4:]]
            kern = functools.partial(_outproj_odd_kernel, final=final)
            out_scratch = []
        x2 = pl.pallas_call(
            kern,
            grid=(bl // tm,),
            in_specs=specs,
            out_specs=tile_spec(d),
            out_shape=jax.ShapeDtypeStruct((bl, d), F32),
            scratch_shapes=out_scratch,
            compiler_params=_cparams(("arbitrary",)),
            name="out_proj",
        )(*ins)
    return x2.reshape(bsz, seq, d)
```

```python
import functools
import math

import jax
import jax.numpy as jnp
from jax import lax
from jax.experimental import pallas as pl
from jax.experimental.pallas import tpu as pltpu

F32 = jnp.float32
BF16 = jnp.bfloat16

NORM_EPS = 1e-6
LANES = 128
SUBLANES = 8
VMEM_LIMIT = 56 * 1024 * 1024

S5_GROUP = 16
S5_STATE = 64
RWKV_HEAD = 64
DECAY_LORA = 64
ICLR_LORA = 64
GATE_LORA = 128
RWKV_DECAY_SCALE = math.exp(-0.5)
GDN_HEAD = 128
GDN_CONV = 4
CHUNK = 64
S5_T = 8

NN = (((1,), (0,)), ((), ()))
NT = (((1,), (1,)), ((), ()))
BNN = (((2,), (1,)), ((0,), (0,)))
BNT = (((2,), (2,)), ((0,), (0,)))


def _split(a, n):
    if a.dtype == BF16:
        return [a]
    parts = []
    r = a
    for i in range(n):
        p = r.astype(BF16)
        parts.append(p)
        if i + 1 < n:
            r = r - p.astype(F32)
    return parts


def _mm(a, b, dn=NN, pa=1, pb=1):
    ap = _split(a, pa)
    bp = _split(b, pb)
    lim = max(len(ap), len(bp))
    acc = None
    for i, x in enumerate(ap):
        for j, y in enumerate(bp):
            if i + j >= lim:
                continue
            t = lax.dot_general(x, y, dn, preferred_element_type=F32)
            acc = t if acc is None else acc + t
    return acc


def _sigmoid(x):
    return 0.5 + 0.5 * jnp.tanh(0.5 * x)


def _silu(x):
    h = 0.5 * x
    return h + h * jnp.tanh(h)


def _softplus(x):
    return jnp.maximum(x, 0.0) + jnp.log(1.0 + jnp.exp(-jnp.abs(x)))


def _gelu_tanh(x):
    c = math.sqrt(2.0 / math.pi)
    return 0.5 * x * (1.0 + jnp.tanh(c * (x + 0.044715 * (x * x * x))))


def _iota2(shape, dim):
    return lax.broadcasted_iota(jnp.int32, shape, dim)


def _inv_unit_lower(n_mat, eye, passes, dn):
    size = n_mat.shape[-1]
    row = _iota2((size, size), 0)
    col = _iota2((size, size), 1)
    t = eye + jnp.where((row // 2) == (col // 2), n_mat, 0.0)
    b = 2
    while b < CHUNK:
        off = ((row // (2 * b)) == (col // (2 * b))) & ((row // b) % 2 == 1) & ((col // b) % 2 == 0)
        x = _mm(jnp.where(off, n_mat, 0.0), t, dn, pa=passes, pb=passes)
        t = t + _mm(t, x, dn, pa=passes, pb=passes)
        b *= 2
    return t


def _cparams(sem):
    return pltpu.CompilerParams(dimension_semantics=sem, vmem_limit_bytes=VMEM_LIMIT)


def _mod_kernel(c_ref, w_ref, b_ref, o_ref):
    s = _silu(c_ref[...])
    o_ref[0] = _mm(s, w_ref[0], pa=2, pb=2) + b_ref[0]


def _modulation(c, ada_w, ada_b):
    depth, d, d3 = ada_w.shape
    bsz = c.shape[0]
    nj = d3 // d
    return pl.pallas_call(
        _mod_kernel,
        grid=(depth, nj),
        in_specs=[
            pl.BlockSpec((bsz, d), lambda l, j: (0, 0)),
            pl.BlockSpec((1, d, d), lambda l, j: (l, 0, j)),
            pl.BlockSpec((1, 1, d), lambda l, j: (l, 0, j)),
        ],
        out_specs=pl.BlockSpec((1, bsz, d), lambda l, j: (l, 0, j)),
        out_shape=jax.ShapeDtypeStruct((depth, bsz, d3), F32),
        compiler_params=_cparams(("arbitrary", "arbitrary")),
        name="adaln_mod",
    )(c, ada_w, ada_b.reshape(depth, 1, d3))


def _norm_mod(x, nw, scale, shift):
    ms = jnp.mean(x * x, axis=-1, keepdims=True)
    h = x * lax.rsqrt(ms + NORM_EPS) * nw
    return h * (1.0 + scale) + shift


def _inproj_even_kernel(x_ref, nw_ref, sh_ref, sc_ref, w_ref, u_ref, f_ref, z_ref, us_ref, *, tn):
    hb = _norm_mod(x_ref[...], nw_ref[...], sc_ref[0], sh_ref[0]).astype(BF16)
    nu = u_ref.shape[0]
    nrow = u_ref.shape[2]
    for j in range(nu):
        us_ref[j] = _mm(hb, w_ref[:, j * LANES:(j + 1) * LANES])
        for t in range(S5_T):
            u_ref[j, 0, :, t * LANES:(t + 1) * LANES] = us_ref[j, pl.ds(t, nrow, stride=S5_T), :]
    off = nu * LANES
    for o_ref in (f_ref, z_ref):
        width = o_ref.shape[1]
        for j0 in range(0, width, tn):
            w = min(tn, width - j0)
            o_ref[:, j0:j0 + w] = _mm(hb, w_ref[:, off + j0:off + j0 + w]).astype(o_ref.dtype)
        off += width


def _inproj_odd_kernel(x_ref, nw_ref, sh_ref, sc_ref, w_ref, qkv_ref, z_ref, ba_ref, *, tn):
    hb = _norm_mod(x_ref[...], nw_ref[...], sc_ref[0], sh_ref[0]).astype(BF16)
    off = 0
    for o_ref in (qkv_ref, z_ref, ba_ref):
        width = o_ref.shape[1]
        for j0 in range(0, width, tn):
            w = min(tn, width - j0)
            o_ref[:, j0:j0 + w] = _mm(hb, w_ref[:, off + j0:off + j0 + w]).astype(o_ref.dtype)
        off += width


def _inproj(x2, nw, shift, scale, w_bf, kernel_fn, out_shapes, out_specs, seq, tm, scratch=(), extra=()):
    bl, d = x2.shape
    n = w_bf.shape[1]
    per_b = seq // tm
    return pl.pallas_call(
        kernel_fn,
        grid=(bl // tm,),
        in_specs=[
            pl.BlockSpec((tm, d), lambda i: (i, 0)),
            pl.BlockSpec((1, d), lambda i: (0, 0)),
            pl.BlockSpec((1, 1, d), lambda i: (i // per_b, 0, 0)),
            pl.BlockSpec((1, 1, d), lambda i: (i // per_b, 0, 0)),
            pl.BlockSpec((d, n), lambda i: (0, 0)),
        ] + [pl.BlockSpec(a.shape, lambda i: (0, 0)) for a in extra],
        out_specs=out_specs,
        out_shape=out_shapes,
        scratch_shapes=list(scratch),
        compiler_params=_cparams(("arbitrary",)),
        name="in_proj",
    )(x2, nw, shift, scale, w_bf, *extra)


def _s5_kernel(u_ref, kd_ref, p_ref, q_ref, are_ref, aim_ref, y_ref, st_ref, sloc_ref, sprev_ref, m_ref, *, tc, nb):
    cb = pl.program_id(1)

    @pl.when(cb == 0)
    def _():
        st_ref[...] = jnp.zeros_like(st_ref)
        for s in range(S5_T):
            for t in range(S5_T):
                blk = kd_ref[t - s, 0].astype(BF16) if t >= s else jnp.zeros((LANES, LANES), BF16)
                m_ref[s * LANES:(s + 1) * LANES, t * LANES:(t + 1) * LANES] = blk

    kcols = u_ref.shape[-1]
    x = u_ref[0].reshape(nb * tc, kcols).astype(BF16)
    y = _mm(x, m_ref[...])
    sloc = _mm(x, p_ref[0])
    nt2 = sloc_ref.shape[0]
    nt = nt2 // 2
    pitch = tc + SUBLANES
    for k in range(nt2):
        for b in range(nb):
            sloc_ref[k, b * pitch:b * pitch + tc, :] = sloc[b * tc:(b + 1) * tc, k * LANES:(k + 1) * LANES]
    a_re = are_ref[0]
    a_im = aim_ref[0]

    def body(c, carry):
        rows = pl.ds(c, nb, stride=pitch)
        new = []
        for k in range(nt):
            s_re, s_im = carry[k], carry[nt + k]
            sprev_ref[k, rows, :] = s_re
            sprev_ref[nt + k, rows, :] = s_im
            ar = a_re[:, k * LANES:(k + 1) * LANES]
            ai = a_im[:, k * LANES:(k + 1) * LANES]
            new.append((ar * s_re - ai * s_im + sloc_ref[k, rows, :],
                        ar * s_im + ai * s_re + sloc_ref[nt + k, rows, :]))
        return tuple(n[0] for n in new) + tuple(n[1] for n in new)

    st = lax.fori_loop(0, tc, body, tuple(st_ref[k] for k in range(nt2)))
    for k in range(nt2):
        st_ref[k] = st[k]
    sprev = jnp.concatenate(
        [jnp.concatenate([sprev_ref[k, b * pitch:b * pitch + tc, :] for b in range(nb)], axis=0)
         for k in range(nt2)], axis=1)
    y = y + _mm(sprev, q_ref[0])
    y_ref[0] = y.reshape(nb, tc, kcols)


def _s5_weights(lambda_re, lambda_im, log_step, b_re, b_im, c_re, c_im):
    g_all, p_all = lambda_re.shape
    t_len = S5_T
    gpt = LANES // S5_GROUP
    nj = g_all // gpt
    lam = lax.complex(jnp.minimum(lambda_re, -1e-4), lambda_im)
    step = jnp.exp(log_step)[:, None]
    lam_bar = jnp.exp(lam * step)
    b_bar = ((lam_bar - 1.0) / lam)[..., None] * lax.complex(b_re, b_im)
    c_mat = lax.complex(c_re, c_im)
    taus = jnp.arange(t_len + 1, dtype=F32)
    pw = jnp.exp((lam * step)[None] * taus[:, None, None])
    hi = lax.Precision.HIGHEST
    kern = jnp.real(jnp.einsum('gcp,tgp,gpd->tgcd', c_mat, pw[:t_len], b_bar, precision=hi))
    s_idx = jnp.arange(t_len)
    eye_g = jnp.eye(gpt, dtype=F32)
    kd = jnp.einsum('tjgoi,hg->tjhigo', kern.reshape(t_len, nj, gpt, S5_GROUP, S5_GROUP), eye_g)
    kd = kd.reshape(t_len, nj, LANES, LANES)
    pb = pw[t_len - 1 - s_idx][:, :, :, None] * b_bar[None]
    pb = jnp.stack([jnp.real(pb), jnp.imag(pb)], axis=0)
    pb = pb.reshape(2, t_len, nj, gpt, p_all, S5_GROUP)
    p_mat = jnp.einsum('rsjgpi,hg->jshirgp', pb, eye_g).reshape(nj, t_len * LANES, 2 * gpt * p_all)
    cq = c_mat[None] * pw[1:t_len + 1][:, :, None, :]
    cq = jnp.stack([jnp.real(cq), -jnp.imag(cq)], axis=0)
    cq = cq.reshape(2, t_len, nj, gpt, S5_GROUP, p_all)
    q_mat = jnp.einsum('rtjgop,hg->jrhptgo', cq, eye_g).reshape(nj, 2 * gpt * p_all, t_len * LANES)
    a_t = pw[t_len].reshape(nj, 1, gpt * p_all)
    return (kd, p_mat.astype(BF16), q_mat.astype(BF16),
            jnp.real(a_t).astype(F32), jnp.imag(a_t).astype(F32))


def _s5_scan(u5, weights):
    kd, p_mat, q_mat, a_re, a_im = weights
    nj, bsz, nchunks, kcols = u5.shape
    tc = min(64, nchunks)
    ns2 = p_mat.shape[-1]
    kern = functools.partial(_s5_kernel, tc=tc, nb=bsz)
    return pl.pallas_call(
        kern,
        grid=(nj, nchunks // tc),
        in_specs=[
            pl.BlockSpec((1, bsz, tc, kcols), lambda j, c: (j, 0, c, 0)),
            pl.BlockSpec((S5_T, 1, LANES, LANES), lambda j, c: (0, j, 0, 0)),
            pl.BlockSpec((1, kcols, ns2), lambda j, c: (j, 0, 0)),
            pl.BlockSpec((1, ns2, kcols), lambda j, c: (j, 0, 0)),
            pl.BlockSpec((1, 1, ns2 // 2), lambda j, c: (j, 0, 0)),
            pl.BlockSpec((1, 1, ns2 // 2), lambda j, c: (j, 0, 0)),
        ],
        out_specs=pl.BlockSpec((1, bsz, tc, kcols), lambda j, c: (j, 0, c, 0)),
        out_shape=jax.ShapeDtypeStruct((nj, bsz, nchunks, kcols), F32),
        scratch_shapes=[
            pltpu.VMEM((ns2 // LANES, bsz, LANES), F32),
            pltpu.VMEM((ns2 // LANES, bsz * (tc + SUBLANES), LANES), F32),
            pltpu.VMEM((ns2 // LANES, bsz * (tc + SUBLANES), LANES), F32),
            pltpu.VMEM((kcols, kcols), BF16),
        ],
        compiler_params=_cparams(("arbitrary", "arbitrary")),
        name="s5_scan",
    )(u5, kd, p_mat, q_mat, a_re, a_im)


def _rwkv_kernel(f_ref, mu_ref, w0_ref, wup_ref, a0_ref, aup_ref, gup_ref, kk_ref, ka_ref, rk_ref,
                 lnw_ref, lnb_ref, ones_ref, o_ref,
                 fbuf, state, a_st, r_st, b_st, k_st, v_st, arw_st, tav_st, rkv_st, arb_st, bkd_st, vk_st,
                 ee_st, o_s, *, lb, inv_passes):
    t = pl.program_id(1)
    nbat = f_ref.shape[0]
    width = o_s.shape[1]
    npair = width // LANES
    nchain = nbat * npair
    nc = lb // CHUNK
    c2 = 2 * CHUNK

    @pl.when(t == 0)
    def _():
        fbuf[:, 0:SUBLANES, :] = jnp.zeros((nbat, SUBLANES, fbuf.shape[2]), F32)
        state[...] = jnp.zeros_like(state)

    @pl.when(t > 0)
    def _():
        fbuf[:, 0:SUBLANES, :] = fbuf[:, lb:lb + SUBLANES, :]

    xl, xsl = [], []
    for bi in range(nbat):
        xb = f_ref[bi]
        fbuf[bi, SUBLANES:SUBLANES + lb, :] = xb
        xl.append(xb)
        xsl.append(fbuf[bi, pl.ds(SUBLANES - 1, lb), :])
    x = jnp.concatenate(xl, axis=0)
    xs = jnp.concatenate(xsl, axis=0)
    f = x + mu_ref[...] * (xs - x)
    r = f[:, 0:width]
    k = f[:, width:2 * width]
    v = f[:, 2 * width:3 * width]
    lora = f[:, 3 * width:3 * width + LANES]
    xg = f[:, 3 * width + LANES:3 * width + 2 * LANES]
    lw = -RWKV_DECAY_SCALE * _sigmoid(w0_ref[...] + _mm(jnp.tanh(lora), wup_ref[...]))
    a = _sigmoid(a0_ref[...] + _mm(lora, aup_ref[...]))
    g = _mm(_sigmoid(xg), gup_ref[...])
    half = width // 2
    ones_half = ones_ref[0:half, 0:half]

    def head_sum(m):
        return jnp.concatenate([_mm(m[:, 0:half], ones_half), _mm(m[:, half:width], ones_half)], axis=1)

    kk = k * kk_ref[...]
    kn = kk * lax.rsqrt(head_sum(kk * kk) + 1e-6)
    kp = k * (1.0 + (a - 1.0) * ka_ref[...])
    bonus = head_sum(r * kp * rk_ref[...]) * v
    bb = kn * a

    row = _iota2((c2, c2), 0)
    col = _iota2((c2, c2), 1)
    same = (row // CHUNK) == (col // CHUNK)
    strict = same & (row > col)
    incl = same & (row >= col)
    eye = jnp.where(row == col, 1.0, 0.0).astype(F32)
    tri = jnp.where(_iota2((CHUNK, CHUNK), 0) >= _iota2((CHUNK, CHUNK), 1), 1.0, 0.0).astype(BF16)
    first = _iota2((CHUNK, LANES), 1) < RWKV_HEAD

    def stack(m):
        return jnp.concatenate([jnp.where(first, m, 0.0), jnp.where(first, 0.0, m)], axis=0)

    for c in range(nc):
        for bi in range(nbat):
            cs = slice(bi * lb + c * CHUNK, bi * lb + (c + 1) * CHUNK)
            lwc = lw[cs]
            gc = _mm(tri, lwc, pb=3)
            e_g = jnp.exp(gc)
            e_ng = jnp.exp(-gc)
            at = -kn[cs] * jnp.exp(gc - lwc)
            bt = bb[cs] * e_ng
            kt = kp[cs] * e_ng
            rt = r[cs] * e_g
            vv = v[cs]
            e_end = jnp.broadcast_to(jnp.exp(gc[CHUNK - 1:CHUNK, :]), (SUBLANES, width))
            for p in range(npair):
                idx = c * nchain + bi * npair + p
                sl = slice(p * LANES, (p + 1) * LANES)
                a_st[idx] = stack(at[:, sl])
                r_st[idx] = stack(rt[:, sl])
                b_st[idx] = stack(bt[:, sl])
                k_st[idx] = stack(kt[:, sl])
                v_st[idx] = stack(vv[:, sl])
                ee_st[idx] = e_end[:, sl]

    a_all = a_st[...]
    r_all = r_st[...]
    v_all = v_st[...]
    ar = jnp.concatenate([a_all, r_all], axis=1)
    bk = jnp.concatenate([b_st[...], k_st[...]], axis=1)
    p4 = _mm(ar, bk, BNT)
    a_ab = jnp.where(strict, p4[:, 0:c2, 0:c2], 0.0)
    a_ak = jnp.where(strict, p4[:, 0:c2, c2:2 * c2], 0.0)
    a_rb = jnp.where(incl, p4[:, c2:2 * c2, 0:c2], 0.0)
    a_rk = jnp.where(incl, p4[:, c2:2 * c2, c2:2 * c2], 0.0)
    t_inv = _inv_unit_lower(a_ab, eye, inv_passes, BNN)
    arw_st[...] = jnp.concatenate([_mm(t_inv, a_all, BNN, pa=inv_passes), r_all], axis=1)
    tav_st[...] = _mm(t_inv, _mm(a_ak, v_all, BNN), BNN, pa=inv_passes)
    rkv_st[...] = _mm(a_rk, v_all, BNN)
    arb_st[...] = a_rb
    bkd = bk * ee_st[:, 0:1, :]
    bkd_st[...] = bkd[:, 0:c2, :]
    vk_st[...] = _mm(jnp.swapaxes(v_all, 1, 2), bkd[:, c2:2 * c2, :], BNN)

    def chunk_body(c, carry):
        xr = [_mm(arw_st[c * nchain + q], state[q], NT) for q in range(nchain)]
        uu = [xr[q][0:c2] + tav_st[c * nchain + q] for q in range(nchain)]
        ob = [_mm(arb_st[c * nchain + q], uu[q]) for q in range(nchain)]
        for q in range(nchain):
            bi, p = divmod(q, npair)
            idx = c * nchain + q
            rows = pl.ds(pl.multiple_of(bi * lb + c * CHUNK, CHUNK), CHUNK)
            o_st = xr[q][c2:2 * c2] + ob[q] + rkv_st[idx]
            o_s[rows, p * LANES:(p + 1) * LANES] = o_st[0:CHUNK] + o_st[CHUNK:c2]
            state[q] = state[q] * ee_st[idx, 0:1, :] + vk_st[idx] + _mm(uu[q].T, bkd_st[idx])
        return carry

    lax.fori_loop(0, nc, chunk_body, 0)

    o = o_s[...]
    inv_n = 1.0 / RWKV_HEAD
    mean = head_sum(o) * inv_n
    d = o - mean
    var = head_sum(d * d) * inv_n
    on = d * lax.rsqrt(var + 1e-5 * RWKV_HEAD) * lnw_ref[...] + lnb_ref[...]
    out = ((on + bonus) * g).astype(o_ref.dtype)
    for bi in range(nbat):
        o_ref[bi] = out[bi * lb:(bi + 1) * lb]


def _rwkv(feats, prm, bsz, seq, lb, inv_passes, nbat):
    bl, fw = feats.shape
    width = prm['w0'].shape[1]
    full = lambda a: pl.BlockSpec(a.shape, lambda i, t: (0, 0))
    names = ['mu', 'w0', 'wup', 'a0', 'aup', 'gup', 'kk', 'ka', 'rk', 'lnw', 'lnb', 'ones']
    per_b = seq // lb
    nchain = nbat * (width // LANES)
    n = (lb // CHUNK) * nchain
    c2 = 2 * CHUNK
    kern = functools.partial(_rwkv_kernel, lb=lb, inv_passes=inv_passes)
    vm = lambda *s: pltpu.VMEM(s, F32)
    out = pl.pallas_call(
        kern,
        grid=(bsz // nbat, per_b),
        in_specs=[pl.BlockSpec((nbat, lb, fw), lambda i, t: (i, t, 0))] + [full(prm[k]) for k in names],
        out_specs=pl.BlockSpec((nbat, lb, width), lambda i, t: (i, t, 0)),
        out_shape=jax.ShapeDtypeStruct((bsz, seq, width), BF16),
        scratch_shapes=[vm(nbat, lb + SUBLANES, fw), vm(nchain, LANES, LANES)]
                       + [vm(n, c2, LANES)] * 5
                       + [vm(n, 2 * c2, LANES), vm(n, c2, LANES), vm(n, c2, LANES), vm(n, c2, c2),
                          vm(n, c2, LANES), vm(n, LANES, LANES), vm(n, SUBLANES, LANES), vm(nbat * lb, width)],
        compiler_params=_cparams(("arbitrary", "arbitrary")),
        name="rwkv7",
    )(feats.reshape(bsz, seq, fw), *[prm[k] for k in names])
    return out.reshape(bl, width)


def _gdn_kernel(qkv_ref, ba_ref, cw_ref, alog_ref, dtb_ref, nw_ref, o_ref,
                cbuf, state, q_st, k_st, v_st, u_st, w_st, qd_st, ket_st, in_st, cd_st, *, lb, nh, inv_passes):
    t = pl.program_id(1)
    nbat = qkv_ref.shape[0]
    hd = GDN_HEAD
    nc = lb // CHUNK
    npair = nh // 2
    npt = nbat * npair
    c2 = 2 * CHUNK
    mixw = nh * hd

    @pl.when(t == 0)
    def _():
        cbuf[:, 0:SUBLANES, :] = jnp.zeros((nbat, SUBLANES, cbuf.shape[2]), F32)
        state[...] = jnp.zeros_like(state)

    @pl.when(t > 0)
    def _():
        cbuf[:, 0:SUBLANES, :] = cbuf[:, lb:lb + SUBLANES, :]

    for bi in range(nbat):
        cbuf[bi, SUBLANES:SUBLANES + lb, :] = qkv_ref[bi]

        def conv_silu(off):
            acc = None
            for j in range(GDN_CONV):
                term = (cw_ref[j:j + 1, off:off + hd]
                        * cbuf[bi, pl.ds(SUBLANES - GDN_CONV + 1 + j, lb), off:off + hd])
                acc = term if acc is None else acc + term
            return _silu(acc)

        for h in range(nh):
            q = conv_silu(h * hd)
            k = conv_silu(mixw + h * hd)
            v = conv_silu(2 * mixw + h * hd)
            q = q * lax.rsqrt(jnp.sum(q * q, axis=-1, keepdims=True) + 1e-6) * (hd ** -0.5)
            k = k * lax.rsqrt(jnp.sum(k * k, axis=-1, keepdims=True) + 1e-6)
            hp, i = divmod(h, 2)
            rs = slice(i * CHUNK, (i + 1) * CHUNK)
            for c in range(nc):
                cs = slice(c * CHUNK, (c + 1) * CHUNK)
                idx = c * npt + bi * npair + hp
                q_st[idx, rs, :] = q[cs]
                k_st[idx, rs, :] = k[cs]
                v_st[idx, rs, :] = v[cs]

    row = _iota2((c2, c2), 0)
    col = _iota2((c2, c2), 1)
    same = (row // CHUNK) == (col // CHUNK)
    strict = same & (row > col)
    incl = same & (row >= col)
    eye = jnp.where(row == col, 1.0, 0.0).astype(F32)
    tri = jnp.where(_iota2((CHUNK, CHUNK), 0) >= _iota2((CHUNK, CHUNK), 1), 1.0, 0.0).astype(BF16)
    lane_first = _iota2((1, c2), 1) < CHUNK
    row_first = _iota2((c2, hd), 0) < CHUNK

    beta_l, ld_l = [], []
    for bi in range(nbat):
        ba = ba_ref[bi]
        beta_l.append(_sigmoid(ba))
        ld_l.append(-jnp.exp(alog_ref[...]) * _softplus(ba + dtb_ref[...]))

    gcols, grows, bcols, gends = [], [], [], []
    for c in range(nc):
        cs = slice(c * CHUNK, (c + 1) * CHUNK)
        for bi in range(nbat):
            g = _mm(tri, ld_l[bi][cs], pb=3)
            g2t = jnp.concatenate([g, g], axis=0).T
            bc = beta_l[bi][cs]
            for hp in range(npair):
                l0 = nh + 2 * hp
                l1 = l0 + 1
                gcols.append(jnp.concatenate([g[:, l0:l0 + 1], g[:, l1:l1 + 1]], axis=0))
                grows.append(jnp.where(lane_first, g2t[l0:l0 + 1, :], g2t[l1:l1 + 1, :]))
                bcols.append(jnp.concatenate([bc[:, 2 * hp:2 * hp + 1], bc[:, 2 * hp + 1:2 * hp + 2]], axis=0))
                gends.append(jnp.concatenate([jnp.broadcast_to(g[CHUNK - 1:CHUNK, l0:l0 + 1], (CHUNK, 1)),
                                              jnp.broadcast_to(g[CHUNK - 1:CHUNK, l1:l1 + 1], (CHUNK, 1))], axis=0))
    gcol = jnp.stack(gcols)
    grow = jnp.stack(grows)
    bcol = jnp.stack(bcols)
    gend = jnp.stack(gends)

    q_all = q_st[...]
    k_all = k_st[...]
    v_all = v_st[...]
    dec = jnp.where(incl, jnp.exp(jnp.where(incl, gcol - grow, 0.0)), 0.0)
    e_g = jnp.exp(gcol)
    kb = k_all * bcol
    a_mat = jnp.where(strict, _mm(kb, k_all, BNT) * dec, 0.0)
    in_st[...] = jnp.where(incl, _mm(q_all, k_all, BNT) * dec, 0.0)
    t_inv = _inv_unit_lower(-a_mat, eye, inv_passes, BNN)
    uw = _mm(t_inv, jnp.concatenate([v_all * bcol, kb * e_g], axis=2), BNN, pa=inv_passes)
    u_st[...] = uw[:, :, 0:hd]
    w_st[...] = uw[:, :, hd:2 * hd]
    qd_st[...] = q_all * e_g
    ket_st[...] = jnp.swapaxes(k_all * jnp.exp(gend - gcol), 1, 2)
    cd_st[...] = jnp.broadcast_to(jnp.exp(gend), cd_st.shape)

    def chunk_body(c, carry):
        rows = pl.ds(pl.multiple_of(c * CHUNK, CHUNK), CHUNK)
        wq = []
        for hh in range(nbat * nh):
            pp, i = divmod(hh, 2)
            rs = slice(i * CHUNK, (i + 1) * CHUNK)
            lhs = jnp.concatenate([w_st[c * npt + pp, rs, :], qd_st[c * npt + pp, rs, :]], axis=0)
            wq.append(_mm(lhs, state[hh]))
        vn = []
        for pp in range(npt):
            w0 = wq[2 * pp]
            w1 = wq[2 * pp + 1]
            vn.append(u_st[c * npt + pp] - jnp.concatenate([w0[0:CHUNK], w1[0:CHUNK]], axis=0))
        oi = [_mm(in_st[c * npt + pp], vn[pp]) for pp in range(npt)]
        for pp in range(npt):
            idx = c * npt + pp
            bi, hp = divmod(pp, npair)
            ket = ket_st[idx]
            for i in range(2):
                hh = 2 * pp + i
                h = 2 * hp + i
                rs = slice(i * CHUNK, (i + 1) * CHUNK)
                keep = row_first if i == 0 else jnp.logical_not(row_first)
                state[hh] = (state[hh] * cd_st[idx, i * CHUNK:i * CHUNK + 1, :]
                             + _mm(ket, jnp.where(keep, vn[pp], 0.0)))
                o_h = wq[hh][CHUNK:c2] + oi[pp][rs]
                o_h = o_h * lax.rsqrt(jnp.mean(o_h * o_h, axis=-1, keepdims=True) + NORM_EPS) * nw_ref[...]
                o_ref[bi, rows, h * hd:(h + 1) * hd] = o_h.astype(o_ref.dtype)
        return carry

    lax.fori_loop(0, nc, chunk_body, 0)


def _gdn(qkv, ba, prm, bsz, seq, lb, inv_passes, nbat):
    bl, w3 = qkv.shape
    mixw = w3 // 3
    nh = mixw // GDN_HEAD
    per_b = seq // lb
    n = (lb // CHUNK) * nbat * (nh // 2)
    c2 = 2 * CHUNK
    full = lambda a: pl.BlockSpec(a.shape, lambda i, t: (0, 0))
    kern = functools.partial(_gdn_kernel, lb=lb, nh=nh, inv_passes=inv_passes)
    vm = lambda *s: pltpu.VMEM(s, F32)
    names = ['cw', 'alog', 'dtb', 'nw']
    out = pl.pallas_call(
        kern,
        grid=(bsz // nbat, per_b),
        in_specs=[pl.BlockSpec((nbat, lb, w3), lambda i, t: (i, t, 0)),
                  pl.BlockSpec((nbat, lb, LANES), lambda i, t: (i, t, 0))] + [full(prm[k]) for k in names],
        out_specs=pl.BlockSpec((nbat, lb, mixw), lambda i, t: (i, t, 0)),
        out_shape=jax.ShapeDtypeStruct((bsz, seq, mixw), BF16),
        scratch_shapes=[vm(nbat, lb + SUBLANES, w3), vm(nbat * nh, GDN_HEAD, GDN_HEAD)]
                       + [vm(n, c2, GDN_HEAD)] * 6
                       + [vm(n, GDN_HEAD, c2), vm(n, c2, c2), vm(n, c2, GDN_HEAD)],
        compiler_params=_cparams(("arbitrary", "arbitrary")),
        name="gdn",
    )(qkv.reshape(bsz, seq, w3), ba.reshape(bsz, seq, LANES), *[prm[k] for k in names])
    return out.reshape(bl, mixw)


def _finish(x_ref, g_ref, acc, fw_ref, o_ref, final):
    xn = x_ref[...] + g_ref[0] * acc
    if final:
        ms = jnp.mean(xn * xn, axis=-1, keepdims=True)
        xn = xn * lax.rsqrt(ms + NORM_EPS) * fw_ref[...]
    o_ref[...] = xn


def _outproj_even_kernel(conv_ref, u_ref, yb_ref, z_ref, x_ref, g_ref, d_ref, gw_ref, gb_ref, w_ref, fw_ref,
                         o_ref, cs_ref, us_ref, *, final):
    nj = conv_ref.shape[0]
    nrow = conv_ref.shape[2]
    for j in range(nj):
        for t in range(S5_T):
            cs_ref[j, pl.ds(t, nrow, stride=S5_T), :] = conv_ref[j, 0, :, t * LANES:(t + 1) * LANES]
            us_ref[j, pl.ds(t, nrow, stride=S5_T), :] = u_ref[j, 0, :, t * LANES:(t + 1) * LANES]
    ya = jnp.concatenate([cs_ref[j] for j in range(nj)], axis=1)
    uu = jnp.concatenate([us_ref[j] for j in range(nj)], axis=1)
    ya = _gelu_tanh(ya + d_ref[...] * uu)
    ya = ya * _sigmoid(_mm(ya, gw_ref[...]) + gb_ref[...])
    wa = ya.shape[1]
    sz = _silu(z_ref[...].astype(F32))
    acc = _mm(ya * sz[:, :wa], w_ref[0:wa, :]) + _mm(yb_ref[...].astype(F32) * sz[:, wa:], w_ref[wa:, :])
    _finish(x_ref, g_ref, acc, fw_ref, o_ref, final)


def _outproj_odd_kernel(y_ref, z_ref, x_ref, g_ref, w_ref, fw_ref, o_ref, *, final):
    acc = _mm(y_ref[...].astype(F32) * _silu(z_ref[...].astype(F32)), w_ref[...])
    _finish(x_ref, g_ref, acc, fw_ref, o_ref, final)


def kernel(x, c, norm_w, ada_w, ada_b, w_out, final_norm_w, even_w_in, s5_lambda_re, s5_lambda_im, s5_log_step, s5_b_re, s5_b_im, s5_c_re, s5_c_im, s5_d, s5_glu_w, s5_glu_b, rwkv_mu, rwkv_w0, rwkv_w_up, rwkv_a0, rwkv_a_up, rwkv_g_up, rwkv_k_k, rwkv_k_a, rwkv_r_k, rwkv_ln_w, rwkv_ln_b, odd_w_in, gdn_conv_w, gdn_a_log, gdn_dt_bias, gdn_norm_w):
    bsz, seq, d = x.shape
    depth = norm_w.shape[0]
    bl = bsz * seq
    tm = min(512, seq)
    lb = min(256, seq)
    inv_passes = 1
    nbat = 2 if bsz % 2 == 0 else 1
    s5w = s5_d.shape[1]
    nj = s5w // LANES
    rw = rwkv_w0.shape[1]
    mixw = w_out.shape[1]
    nh_gdn = mixw // GDN_HEAD
    per_b = seq // tm

    mod = _modulation(c, ada_w, ada_b)
    x2 = x.reshape(bl, d)
    row1 = lambda a: a.reshape(1, -1)
    tile_spec = lambda w: pl.BlockSpec((tm, w), lambda i: (i, 0))
    res_spec = lambda a: pl.BlockSpec(a.shape, lambda i: tuple(0 for _ in a.shape))
    gate_spec = pl.BlockSpec((1, 1, d), lambda i: (i // per_b, 0, 0))
    fw = row1(final_norm_w)

    for layer in range(depth):
        shift = mod[layer, :, 0:d].reshape(bsz, 1, d)
        scale = mod[layer, :, d:2 * d].reshape(bsz, 1, d)
        gate = mod[layer, :, 2 * d:3 * d].reshape(bsz, 1, d)
        i = layer // 2
        final = layer == depth - 1
        nw = row1(norm_w[layer])
        wo = w_out[layer].astype(BF16)
        if layer % 2 == 0:
            fwid = even_w_in.shape[2] - s5w - mixw
            s5_spec = pl.BlockSpec((nj, 1, tm // S5_T, S5_T * LANES), lambda r: (0, r // per_b, r % per_b, 0))
            s5_scratch = pltpu.VMEM((nj, tm, LANES), F32)
            u5, feats, z = _inproj(
                x2, nw, shift, scale, even_w_in[i].astype(BF16),
                functools.partial(_inproj_even_kernel, tn=512),
                (jax.ShapeDtypeStruct((nj, bsz, seq // S5_T, S5_T * LANES), F32),
                 jax.ShapeDtypeStruct((bl, fwid), F32), jax.ShapeDtypeStruct((bl, mixw), BF16)),
                (s5_spec, tile_spec(fwid), tile_spec(mixw)),
                seq, tm, scratch=(s5_scratch,))
            conv5 = _s5_scan(u5, _s5_weights(s5_lambda_re[i], s5_lambda_im[i], s5_log_step[i], s5_b_re[i],
                                             s5_b_im[i], s5_c_re[i], s5_c_im[i]))
            zpad = jnp.zeros((LANES - DECAY_LORA, rw), F32)
            ones_bd = jnp.kron(jnp.eye(rw // RWKV_HEAD, dtype=F32), jnp.ones((RWKV_HEAD, RWKV_HEAD), F32))
            prm = dict(mu=row1(rwkv_mu[i]), w0=row1(rwkv_w0[i]),
                       wup=jnp.concatenate([rwkv_w_up[i], zpad], axis=0).astype(BF16),
                       a0=row1(rwkv_a0[i]),
                       aup=jnp.concatenate([zpad, rwkv_a_up[i]], axis=0).astype(BF16),
                       gup=rwkv_g_up[i].astype(BF16), kk=row1(rwkv_k_k[i]), ka=row1(rwkv_k_a[i]),
                       rk=row1(rwkv_r_k[i]), lnw=row1(rwkv_ln_w[i]), lnb=row1(rwkv_ln_b[i]),
                       ones=ones_bd.astype(BF16))
            yb = _rwkv(feats, prm, bsz, seq, lb, inv_passes, nbat)
            ins = (conv5, u5, yb, z, x2, gate, row1(s5_d[i]), s5_glu_w[i].astype(BF16), row1(s5_glu_b[i]), wo, fw)
            specs = [s5_spec, s5_spec,
                     tile_spec(rw), tile_spec(mixw), tile_spec(d), gate_spec] + [res_spec(a) for a in ins[6:]]
            kern = functools.partial(_outproj_even_kernel, final=final)
            out_scratch = [s5_scratch, s5_scratch]
        else:
            w_in = odd_w_in[i]
            q_end = 3 * mixw
            ba_w = jnp.concatenate([w_in[:, q_end:q_end + 2 * nh_gdn],
                                    jnp.zeros((d, LANES - 2 * nh_gdn), F32)], axis=1)
            w_r = jnp.concatenate([w_in[:, :q_end], w_in[:, q_end + 2 * nh_gdn:], ba_w], axis=1).astype(BF16)
            qkv, z, ba = _inproj(
                x2, nw, shift, scale, w_r,
                functools.partial(_inproj_odd_kernel, tn=512),
                (jax.ShapeDtypeStruct((bl, q_end), F32), jax.ShapeDtypeStruct((bl, mixw), BF16),
                 jax.ShapeDtypeStruct((bl, LANES), F32)),
                (tile_spec(q_end), tile_spec(mixw), tile_spec(LANES)),
                seq, tm)
            pad_row = lambda a: jnp.zeros((1, LANES), F32).at[0, nh_gdn:2 * nh_gdn].set(a)
            prm = dict(cw=gdn_conv_w[i], alog=pad_row(gdn_a_log[i]), dtb=pad_row(gdn_dt_bias[i]),
                       nw=row1(gdn_norm_w[i]))
            y = _gdn(qkv, ba, prm, bsz, seq, lb, inv_passes, nbat)
            ins = (y, z, x2, gate, wo, fw)
            specs = [tile_spec(mixw), tile_spec(mixw), tile_spec(d), gate_spec] + [res_spec(a) for a in ins[4:]]
            kern = functools.partial(_outproj_odd_kernel, final=final)
            out_scratch = []
        x2 = pl.pallas_call(
            kern,
            grid=(bl // tm,),
            in_specs=specs,
            out_specs=tile_spec(d),
            out_shape=jax.ShapeDtypeStruct((bl, d), F32),
            scratch_shapes=out_scratch,
            compiler_params=_cparams(("arbitrary",)),
            name="out_proj",
        )(*ins)
    return x2.reshape(bsz, seq, d)
```

```python
import functools
import math

import jax
import jax.numpy as jnp
from jax import lax
from jax.experimental import pallas as pl
from jax.experimental.pallas import tpu as pltpu

F32 = jnp.float32
BF16 = jnp.bfloat16

NORM_EPS = 1e-6
LANES = 128
SUBLANES = 8
VMEM_LIMIT = 56 * 1024 * 1024

S5_GROUP = 16
S5_STATE = 64
RWKV_HEAD = 64
DECAY_LORA = 64
ICLR_LORA = 64
GATE_LORA = 128
RWKV_DECAY_SCALE = math.exp(-0.5)
GDN_HEAD = 128
GDN_CONV = 4
CHUNK = 64
S5_T = 8

NN = (((1,), (0,)), ((), ()))
NT = (((1,), (1,)), ((), ()))
BNN = (((2,), (1,)), ((0,), (0,)))
BNT = (((2,), (2,)), ((0,), (0,)))


def _split(a, n):
    if a.dtype == BF16:
        return [a]
    parts = []
    r = a
    for i in range(n):
        p = r.astype(BF16)
        parts.append(p)
        if i + 1 < n:
            r = r - p.astype(F32)
    return parts


def _mm(a, b, dn=NN, pa=1, pb=1):
    ap = _split(a, pa)
    bp = _split(b, pb)
    lim = max(len(ap), len(bp))
    acc = None
    for i, x in enumerate(ap):
        for j, y in enumerate(bp):
            if i + j >= lim:
                continue
            t = lax.dot_general(x, y, dn, preferred_element_type=F32)
            acc = t if acc is None else acc + t
    return acc


def _sigmoid(x):
    return 0.5 + 0.5 * jnp.tanh(0.5 * x)


def _silu(x):
    h = 0.5 * x
    return h + h * jnp.tanh(h)


def _softplus(x):
    return jnp.maximum(x, 0.0) + jnp.log(1.0 + jnp.exp(-jnp.abs(x)))


def _gelu_tanh(x):
    c = math.sqrt(2.0 / math.pi)
    return 0.5 * x * (1.0 + jnp.tanh(c * (x + 0.044715 * (x * x * x))))


def _iota2(shape, dim):
    return lax.broadcasted_iota(jnp.int32, shape, dim)


def _inv_unit_lower(n_mat, eye, passes, dn, row, col):
    t = eye + jnp.where((row // 2) == (col // 2), n_mat, 0.0)
    b = 2
    while b < CHUNK:
        off = ((row // (2 * b)) == (col // (2 * b))) & ((row // b) % 2 == 1) & ((col // b) % 2 == 0)
        x = _mm(jnp.where(off, n_mat, 0.0), t, dn, pa=passes, pb=passes)
        t = t + _mm(t, x, dn, pa=passes, pb=passes)
        b *= 2
    return t


def _cparams(sem):
    return pltpu.CompilerParams(dimension_semantics=sem, vmem_limit_bytes=VMEM_LIMIT)


def _mod_kernel(c_ref, w_ref, b_ref, o_ref):
    s = _silu(c_ref[...])
    o_ref[0] = _mm(s, w_ref[0], pa=2, pb=2) + b_ref[0]


def _modulation(c, ada_w, ada_b):
    depth, d, d3 = ada_w.shape
    bsz = c.shape[0]
    nj = d3 // d
    return pl.pallas_call(
        _mod_kernel,
        grid=(depth, nj),
        in_specs=[
            pl.BlockSpec((bsz, d), lambda l, j: (0, 0)),
            pl.BlockSpec((1, d, d), lambda l, j: (l, 0, j)),
            pl.BlockSpec((1, 1, d), lambda l, j: (l, 0, j)),
        ],
        out_specs=pl.BlockSpec((1, bsz, d), lambda l, j: (l, 0, j)),
        out_shape=jax.ShapeDtypeStruct((depth, bsz, d3), F32),
        compiler_params=_cparams(("arbitrary", "arbitrary")),
        name="adaln_mod",
    )(c, ada_w, ada_b.reshape(depth, 1, d3))


def _norm_mod(x, nw, scale, shift):
    ms = jnp.mean(x * x, axis=-1, keepdims=True)
    h = x * lax.rsqrt(ms + NORM_EPS) * nw
    return h * (1.0 + scale) + shift


def _inproj_even_kernel(x_ref, nw_ref, sh_ref, sc_ref, w_ref, u_ref, f_ref, z_ref, us_ref, *, tn):
    hb = _norm_mod(x_ref[...], nw_ref[...], sc_ref[0], sh_ref[0]).astype(BF16)
    nu = u_ref.shape[0]
    nrow = u_ref.shape[2]
    for j in range(nu):
        us_ref[j] = _mm(hb, w_ref[:, j * LANES:(j + 1) * LANES])
        for t in range(S5_T):
            u_ref[j, 0, :, t * LANES:(t + 1) * LANES] = (
                us_ref[j, pl.ds(t, nrow, stride=S5_T), :].astype(u_ref.dtype))
    off = nu * LANES
    for o_ref in (f_ref, z_ref):
        width = o_ref.shape[1]
        for j0 in range(0, width, tn):
            w = min(tn, width - j0)
            o_ref[:, j0:j0 + w] = _mm(hb, w_ref[:, off + j0:off + j0 + w]).astype(o_ref.dtype)
        off += width


def _inproj_odd_kernel(x_ref, nw_ref, sh_ref, sc_ref, w_ref, qkv_ref, z_ref, ba_ref, *, tn):
    hb = _norm_mod(x_ref[...], nw_ref[...], sc_ref[0], sh_ref[0]).astype(BF16)
    off = 0
    for o_ref in (qkv_ref, z_ref, ba_ref):
        width = o_ref.shape[1]
        for j0 in range(0, width, tn):
            w = min(tn, width - j0)
            o_ref[:, j0:j0 + w] = _mm(hb, w_ref[:, off + j0:off + j0 + w]).astype(o_ref.dtype)
        off += width


def _inproj(x2, nw, shift, scale, w_bf, kernel_fn, out_shapes, out_specs, seq, tm, scratch=(), extra=()):
    bl, d = x2.shape
    n = w_bf.shape[1]
    per_b = seq // tm
    return pl.pallas_call(
        kernel_fn,
        grid=(bl // tm,),
        in_specs=[
            pl.BlockSpec((tm, d), lambda i: (i, 0)),
            pl.BlockSpec((1, d), lambda i: (0, 0)),
            pl.BlockSpec((1, 1, d), lambda i: (i // per_b, 0, 0)),
            pl.BlockSpec((1, 1, d), lambda i: (i // per_b, 0, 0)),
            pl.BlockSpec((d, n), lambda i: (0, 0)),
        ] + [pl.BlockSpec(a.shape, lambda i: (0, 0)) for a in extra],
        out_specs=out_specs,
        out_shape=out_shapes,
        scratch_shapes=list(scratch),
        compiler_params=_cparams(("arbitrary",)),
        name="in_proj",
    )(x2, nw, shift, scale, w_bf, *extra)


def _s5_kernel(u_ref, kd_ref, p_ref, q_ref, are_ref, aim_ref, y_ref, st_ref, sloc_ref, sprev_ref, m_ref, *, tc, nb):
    cb = pl.program_id(1)

    @pl.when(cb == 0)
    def _():
        st_ref[...] = jnp.zeros_like(st_ref)
        for s in range(S5_T):
            for t in range(S5_T):
                blk = kd_ref[t - s, 0].astype(BF16) if t >= s else jnp.zeros((LANES, LANES), BF16)
                m_ref[s * LANES:(s + 1) * LANES, t * LANES:(t + 1) * LANES] = blk

    kcols = u_ref.shape[-1]
    x = u_ref[0].reshape(nb * tc, kcols).astype(BF16)
    y = _mm(x, m_ref[...])
    sloc = _mm(x, p_ref[0])
    nt2 = sloc_ref.shape[0]
    nt = nt2 // 2
    pitch = tc + SUBLANES
    for k in range(nt2):
        for b in range(nb):
            sloc_ref[k, b * pitch:b * pitch + tc, :] = sloc[b * tc:(b + 1) * tc, k * LANES:(k + 1) * LANES]
    a_re = are_ref[0]
    a_im = aim_ref[0]

    def body(c, carry):
        rows = pl.ds(c, nb, stride=pitch)
        new = []
        for k in range(nt):
            s_re, s_im = carry[k], carry[nt + k]
            sprev_ref[k, rows, :] = s_re
            sprev_ref[nt + k, rows, :] = s_im
            ar = a_re[:, k * LANES:(k + 1) * LANES]
            ai = a_im[:, k * LANES:(k + 1) * LANES]
            new.append((ar * s_re - ai * s_im + sloc_ref[k, rows, :],
                        ar * s_im + ai * s_re + sloc_ref[nt + k, rows, :]))
        return tuple(n[0] for n in new) + tuple(n[1] for n in new)

    st = lax.fori_loop(0, tc, body, tuple(st_ref[k] for k in range(nt2)))
    for k in range(nt2):
        st_ref[k] = st[k]
    sprev = jnp.concatenate(
        [jnp.concatenate([sprev_ref[k, b * pitch:b * pitch + tc, :] for b in range(nb)], axis=0)
         for k in range(nt2)], axis=1)
    y = y + _mm(sprev, q_ref[0])
    y_ref[0] = y.reshape(nb, tc, kcols).astype(y_ref.dtype)


def _s5_weights(lambda_re, lambda_im, log_step, b_re, b_im, c_re, c_im):
    g_all, p_all = lambda_re.shape
    t_len = S5_T
    gpt = LANES // S5_GROUP
    nj = g_all // gpt
    lam = lax.complex(jnp.minimum(lambda_re, -1e-4), lambda_im)
    step = jnp.exp(log_step)[:, None]
    lam_bar = jnp.exp(lam * step)
    b_bar = ((lam_bar - 1.0) / lam)[..., None] * lax.complex(b_re, b_im)
    c_mat = lax.complex(c_re, c_im)
    taus = jnp.arange(t_len + 1, dtype=F32)
    pw = jnp.exp((lam * step)[None] * taus[:, None, None])
    hi = lax.Precision.HIGHEST
    kern = jnp.real(jnp.einsum('gcp,tgp,gpd->tgcd', c_mat, pw[:t_len], b_bar, precision=hi))
    s_idx = jnp.arange(t_len)
    eye_g = jnp.eye(gpt, dtype=F32)
    kd = jnp.einsum('tjgoi,hg->tjhigo', kern.reshape(t_len, nj, gpt, S5_GROUP, S5_GROUP), eye_g)
    kd = kd.reshape(t_len, nj, LANES, LANES)
    pb = pw[t_len - 1 - s_idx][:, :, :, None] * b_bar[None]
    pb = jnp.stack([jnp.real(pb), jnp.imag(pb)], axis=0)
    pb = pb.reshape(2, t_len, nj, gpt, p_all, S5_GROUP)
    p_mat = jnp.einsum('rsjgpi,hg->jshirgp', pb, eye_g).reshape(nj, t_len * LANES, 2 * gpt * p_all)
    cq = c_mat[None] * pw[1:t_len + 1][:, :, None, :]
    cq = jnp.stack([jnp.real(cq), -jnp.imag(cq)], axis=0)
    cq = cq.reshape(2, t_len, nj, gpt, S5_GROUP, p_all)
    q_mat = jnp.einsum('rtjgop,hg->jrhptgo', cq, eye_g).reshape(nj, 2 * gpt * p_all, t_len * LANES)
    a_t = pw[t_len].reshape(nj, 1, gpt * p_all)
    return (kd, p_mat.astype(BF16), q_mat.astype(BF16),
            jnp.real(a_t).astype(F32), jnp.imag(a_t).astype(F32))


def _s5_scan(u5, weights):
    kd, p_mat, q_mat, a_re, a_im = weights
    nj, bsz, nchunks, kcols = u5.shape
    tc = min(64, nchunks)
    ns2 = p_mat.shape[-1]
    kern = functools.partial(_s5_kernel, tc=tc, nb=bsz)
    return pl.pallas_call(
        kern,
        grid=(nj, nchunks // tc),
        in_specs=[
            pl.BlockSpec((1, bsz, tc, kcols), lambda j, c: (j, 0, c, 0)),
            pl.BlockSpec((S5_T, 1, LANES, LANES), lambda j, c: (0, j, 0, 0)),
            pl.BlockSpec((1, kcols, ns2), lambda j, c: (j, 0, 0)),
            pl.BlockSpec((1, ns2, kcols), lambda j, c: (j, 0, 0)),
            pl.BlockSpec((1, 1, ns2 // 2), lambda j, c: (j, 0, 0)),
            pl.BlockSpec((1, 1, ns2 // 2), lambda j, c: (j, 0, 0)),
        ],
        out_specs=pl.BlockSpec((1, bsz, tc, kcols), lambda j, c: (j, 0, c, 0)),
        out_shape=jax.ShapeDtypeStruct((nj, bsz, nchunks, kcols), BF16),
        scratch_shapes=[
            pltpu.VMEM((ns2 // LANES, bsz, LANES), F32),
            pltpu.VMEM((ns2 // LANES, bsz * (tc + SUBLANES), LANES), F32),
            pltpu.VMEM((ns2 // LANES, bsz * (tc + SUBLANES), LANES), F32),
            pltpu.VMEM((kcols, kcols), BF16),
        ],
        compiler_params=_cparams(("arbitrary", "arbitrary")),
        name="s5_scan",
    )(u5, kd, p_mat, q_mat, a_re, a_im)


def _rwkv_kernel(f_ref, mu_ref, w0_ref, wup_ref, a0_ref, aup_ref, gup_ref, kk_ref, ka_ref, rk_ref,
                 lnw_ref, lnb_ref, ones_ref, o_ref,
                 fbuf, state, a_st, r_st, b_st, k_st, v_st, arw_st, tav_st, rkv_st, arb_st, bkd_st, vk_st,
                 ee_st, o_s, *, lb, inv_passes):
    t = pl.program_id(1)
    nbat = f_ref.shape[0]
    width = o_s.shape[1]
    npair = width // LANES
    nchain = nbat * npair
    nc = lb // CHUNK
    c2 = 2 * CHUNK

    @pl.when(t == 0)
    def _():
        fbuf[:, 0:SUBLANES, :] = jnp.zeros((nbat, SUBLANES, fbuf.shape[2]), F32)
        state[...] = jnp.zeros_like(state)

    @pl.when(t > 0)
    def _():
        fbuf[:, 0:SUBLANES, :] = fbuf[:, lb:lb + SUBLANES, :]

    xl, xsl = [], []
    for bi in range(nbat):
        xb = f_ref[bi]
        fbuf[bi, SUBLANES:SUBLANES + lb, :] = xb
        xl.append(xb)
        xsl.append(fbuf[bi, pl.ds(SUBLANES - 1, lb), :])
    x = jnp.concatenate(xl, axis=0)
    xs = jnp.concatenate(xsl, axis=0)
    f = x + mu_ref[...] * (xs - x)
    r = f[:, 0:width]
    k = f[:, width:2 * width]
    v = f[:, 2 * width:3 * width]
    lora = f[:, 3 * width:3 * width + LANES]
    xg = f[:, 3 * width + LANES:3 * width + 2 * LANES]
    lw = -RWKV_DECAY_SCALE * _sigmoid(w0_ref[...] + _mm(jnp.tanh(lora), wup_ref[...]))
    a = _sigmoid(a0_ref[...] + _mm(lora, aup_ref[...]))
    g = _mm(_sigmoid(xg), gup_ref[...])
    half = width // 2
    ones_half = ones_ref[0:half, 0:half]

    def head_sum(m):
        return jnp.concatenate([_mm(m[:, 0:half], ones_half), _mm(m[:, half:width], ones_half)], axis=1)

    kk = k * kk_ref[...]
    kn = kk * lax.rsqrt(head_sum(kk * kk) + 1e-6)
    kp = k * (1.0 + (a - 1.0) * ka_ref[...])
    bonus = head_sum(r * kp * rk_ref[...]) * v
    bb = kn * a

    row = _iota2((c2, c2), 0)
    col = _iota2((c2, c2), 1)
    same = (row // CHUNK) == (col // CHUNK)
    strict = same & (row > col)
    incl = same & (row >= col)
    eye = jnp.where(row == col, 1.0, 0.0).astype(F32)
    tri = jnp.where(_iota2((CHUNK, CHUNK), 0) >= _iota2((CHUNK, CHUNK), 1), 1.0, 0.0).astype(BF16)
    first = _iota2((CHUNK, LANES), 1) < RWKV_HEAD

    def stack(m):
        return jnp.concatenate([jnp.where(first, m, 0.0), jnp.where(first, 0.0, m)], axis=0)

    for c in range(nc):
        for bi in range(nbat):
            cs = slice(bi * lb + c * CHUNK, bi * lb + (c + 1) * CHUNK)
            lwc = lw[cs]
            gc = _mm(tri, lwc, pb=3)
            e_g = jnp.exp(gc)
            e_ng = jnp.exp(-gc)
            at = -kn[cs] * jnp.exp(gc - lwc)
            bt = bb[cs] * e_ng
            kt = kp[cs] * e_ng
            rt = r[cs] * e_g
            vv = v[cs]
            e_end = jnp.broadcast_to(jnp.exp(gc[CHUNK - 1:CHUNK, :]), (SUBLANES, width))
            for p in range(npair):
                idx = c * nchain + bi * npair + p
                sl = slice(p * LANES, (p + 1) * LANES)
                a_st[idx] = stack(at[:, sl])
                r_st[idx] = stack(rt[:, sl])
                b_st[idx] = stack(bt[:, sl])
                k_st[idx] = stack(kt[:, sl])
                v_st[idx] = stack(vv[:, sl])
                ee_st[idx] = e_end[:, sl]

    a_all = a_st[...]
    r_all = r_st[...]
    v_all = v_st[...]
    ar = jnp.concatenate([a_all, r_all], axis=1)
    bk = jnp.concatenate([b_st[...], k_st[...]], axis=1)
    p4 = _mm(ar, bk, BNT)
    a_ab = jnp.where(strict, p4[:, 0:c2, 0:c2], 0.0)
    a_ak = jnp.where(strict, p4[:, 0:c2, c2:2 * c2], 0.0)
    a_rb = jnp.where(incl, p4[:, c2:2 * c2, 0:c2], 0.0)
    a_rk = jnp.where(incl, p4[:, c2:2 * c2, c2:2 * c2], 0.0)
    t_inv = _inv_unit_lower(a_ab, eye, inv_passes, BNN, row, col)
    arw_st[...] = jnp.concatenate([_mm(t_inv, a_all, BNN, pa=inv_passes), r_all], axis=1)
    tav_st[...] = _mm(t_inv, _mm(a_ak, v_all, BNN), BNN, pa=inv_passes)
    rkv_st[...] = _mm(a_rk, v_all, BNN)
    arb_st[...] = a_rb
    bkd = bk * ee_st[:, 0:1, :]
    bkd_st[...] = bkd[:, 0:c2, :]
    vk_st[...] = _mm(jnp.swapaxes(v_all, 1, 2), bkd[:, c2:2 * c2, :], BNN)

    def chunk_body(c, carry):
        xr = [_mm(arw_st[c * nchain + q], state[q], NT) for q in range(nchain)]
        uu = [xr[q][0:c2] + tav_st[c * nchain + q] for q in range(nchain)]
        ob = [_mm(arb_st[c * nchain + q], uu[q]) for q in range(nchain)]
        for q in range(nchain):
            bi, p = divmod(q, npair)
            idx = c * nchain + q
            rows = pl.ds(pl.multiple_of(bi * lb + c * CHUNK, CHUNK), CHUNK)
            o_st = xr[q][c2:2 * c2] + ob[q] + rkv_st[idx]
            o_s[rows, p * LANES:(p + 1) * LANES] = o_st[0:CHUNK] + o_st[CHUNK:c2]
            state[q] = state[q] * ee_st[idx, 0:1, :] + vk_st[idx] + _mm(uu[q].T, bkd_st[idx])
        return carry

    lax.fori_loop(0, nc, chunk_body, 0)

    o = o_s[...]
    inv_n = 1.0 / RWKV_HEAD
    mean = head_sum(o) * inv_n
    d = o - mean
    var = head_sum(d * d) * inv_n
    on = d * lax.rsqrt(var + 1e-5 * RWKV_HEAD) * lnw_ref[...] + lnb_ref[...]
    out = ((on + bonus) * g).astype(o_ref.dtype)
    for bi in range(nbat):
        o_ref[bi] = out[bi * lb:(bi + 1) * lb]


def _rwkv(feats, prm, bsz, seq, lb, inv_passes, nbat):
    bl, fw = feats.shape
    width = prm['w0'].shape[1]
    full = lambda a: pl.BlockSpec(a.shape, lambda i, t: (0, 0))
    names = ['mu', 'w0', 'wup', 'a0', 'aup', 'gup', 'kk', 'ka', 'rk', 'lnw', 'lnb', 'ones']
    per_b = seq // lb
    nchain = nbat * (width // LANES)
    n = (lb // CHUNK) * nchain
    c2 = 2 * CHUNK
    kern = functools.partial(_rwkv_kernel, lb=lb, inv_passes=inv_passes)
    vm = lambda *s: pltpu.VMEM(s, F32)
    out = pl.pallas_call(
        kern,
        grid=(bsz // nbat, per_b),
        in_specs=[pl.BlockSpec((nbat, lb, fw), lambda i, t: (i, t, 0))] + [full(prm[k]) for k in names],
        out_specs=pl.BlockSpec((nbat, lb, width), lambda i, t: (i, t, 0)),
        out_shape=jax.ShapeDtypeStruct((bsz, seq, width), BF16),
        scratch_shapes=[vm(nbat, lb + SUBLANES, fw), vm(nchain, LANES, LANES)]
                       + [vm(n, c2, LANES)] * 5
                       + [vm(n, 2 * c2, LANES), vm(n, c2, LANES), vm(n, c2, LANES), vm(n, c2, c2),
                          vm(n, c2, LANES), vm(n, LANES, LANES), vm(n, SUBLANES, LANES), vm(nbat * lb, width)],
        compiler_params=_cparams(("arbitrary", "arbitrary")),
        name="rwkv7",
    )(feats.reshape(bsz, seq, fw), *[prm[k] for k in names])
    return out.reshape(bl, width)


def _gdn_kernel(qkv_ref, ba_ref, cw_ref, alog_ref, dtb_ref, nw_ref, o_ref,
                cbuf, state, q_st, k_st, v_st, u_st, w_st, qd_st, ket_st, in_st, cd_st, o_s, *, lb, nh, inv_passes):
    t = pl.program_id(1)
    nbat = qkv_ref.shape[0]
    hd = GDN_HEAD
    nc = lb // CHUNK
    npair = nh // 2
    npt = nbat * npair
    c2 = 2 * CHUNK
    ntile = 3 * nh
    nrow = lb // SUBLANES
    per = CHUNK // SUBLANES

    @pl.when(t == 0)
    def _():
        cbuf[:, 0:SUBLANES, :] = jnp.zeros((nbat * ntile, SUBLANES, LANES), F32)
        state[...] = jnp.zeros_like(state)

    @pl.when(t > 0)
    def _():
        cbuf[:, 0:SUBLANES, :] = cbuf[:, lb:lb + SUBLANES, :]

    for bi in range(nbat):
        for lt in range(ntile):
            ct = bi * ntile + lt
            cbuf[ct, SUBLANES:SUBLANES + lb, :] = qkv_ref[bi, :, lt * LANES:(lt + 1) * LANES]
            slab = {s: cbuf[ct, pl.ds(SUBLANES + s, nrow, stride=SUBLANES), :]
                    for s in range(1 - GDN_CONV, SUBLANES)}
            cwt = [cw_ref[j:j + 1, lt * LANES:(lt + 1) * LANES] for j in range(GDN_CONV)]
            kind, h = divmod(lt, nh)
            dst = (q_st, k_st, v_st)[kind]
            hp, i = divmod(h, 2)
            for s in range(SUBLANES):
                acc = cwt[0] * slab[s + 1 - GDN_CONV]
                for j in range(1, GDN_CONV):
                    acc = acc + cwt[j] * slab[s + 1 - GDN_CONV + j]
                y = _silu(acc)
                if kind == 0:
                    y = y * (lax.rsqrt(jnp.sum(y * y, axis=-1, keepdims=True) + 1e-6) * (hd ** -0.5))
                elif kind == 1:
                    y = y * lax.rsqrt(jnp.sum(y * y, axis=-1, keepdims=True) + 1e-6)
                for c in range(nc):
                    r0 = i * CHUNK + s * per
                    dst[c * npt + bi * npair + hp, r0:r0 + per, :] = y[c * per:(c + 1) * per]

    def time_of(idx):
        within = idx % CHUNK
        return (idx // CHUNK) * CHUNK + (within % per) * SUBLANES + within // per

    row = time_of(_iota2((c2, c2), 0))
    col = time_of(_iota2((c2, c2), 1))
    same = (row // CHUNK) == (col // CHUNK)
    strict = same & (row > col)
    incl = same & (row >= col)
    eye = jnp.where(row == col, 1.0, 0.0).astype(F32)
    tri = jnp.where(time_of(_iota2((CHUNK, CHUNK), 0)) >= time_of(_iota2((CHUNK, CHUNK), 1)), 1.0, 0.0).astype(BF16)
    lane_first = _iota2((1, c2), 1) < CHUNK
    row_first = _iota2((c2, hd), 0) < CHUNK

    gcols, grows, bcols, gends = [], [], [], []
    for c in range(nc):
        for bi in range(nbat):
            ba = jnp.concatenate([ba_ref[bi, pl.ds(c * CHUNK + s, per, stride=SUBLANES), :]
                                  for s in range(SUBLANES)], axis=0)
            bc = _sigmoid(ba)
            ld = -jnp.exp(alog_ref[...]) * _softplus(ba + dtb_ref[...])
            g = _mm(tri, ld, pb=3)
            g2t = jnp.concatenate([g, g], axis=0).T
            for hp in range(npair):
                l0 = nh + 2 * hp
                l1 = l0 + 1
                gcols.append(jnp.concatenate([g[:, l0:l0 + 1], g[:, l1:l1 + 1]], axis=0))
                grows.append(jnp.where(lane_first, g2t[l0:l0 + 1, :], g2t[l1:l1 + 1, :]))
                bcols.append(jnp.concatenate([bc[:, 2 * hp:2 * hp + 1], bc[:, 2 * hp + 1:2 * hp + 2]], axis=0))
                gends.append(jnp.concatenate([jnp.broadcast_to(g[CHUNK - 1:CHUNK, l0:l0 + 1], (CHUNK, 1)),
                                              jnp.broadcast_to(g[CHUNK - 1:CHUNK, l1:l1 + 1], (CHUNK, 1))], axis=0))
    gcol = jnp.stack(gcols)
    grow = jnp.stack(grows)
    bcol = jnp.stack(bcols)
    gend = jnp.stack(gends)

    q_all = q_st[...]
    k_all = k_st[...]
    v_all = v_st[...]
    dec = jnp.where(incl, jnp.exp(jnp.where(incl, gcol - grow, 0.0)), 0.0)
    e_g = jnp.exp(gcol)
    kb = k_all * bcol
    a_mat = jnp.where(strict, _mm(kb, k_all, BNT) * dec, 0.0)
    in_st[...] = jnp.where(incl, _mm(q_all, k_all, BNT) * dec, 0.0)
    t_inv = _inv_unit_lower(-a_mat, eye, inv_passes, BNN, row, col)
    uw = _mm(t_inv, jnp.concatenate([v_all * bcol, kb * e_g], axis=2), BNN, pa=inv_passes)
    u_st[...] = uw[:, :, 0:hd]
    w_st[...] = uw[:, :, hd:2 * hd]
    qd_st[...] = q_all * e_g
    ket_st[...] = jnp.swapaxes(k_all * jnp.exp(gend - gcol), 1, 2)
    cd_st[...] = jnp.broadcast_to(jnp.exp(gend), cd_st.shape)

    def chunk_body(c, carry):
        wq = []
        for hh in range(nbat * nh):
            pp, i = divmod(hh, 2)
            rs = slice(i * CHUNK, (i + 1) * CHUNK)
            lhs = jnp.concatenate([w_st[c * npt + pp, rs, :], qd_st[c * npt + pp, rs, :]], axis=0)
            wq.append(_mm(lhs, state[hh]))
        vn = []
        for pp in range(npt):
            w0 = wq[2 * pp]
            w1 = wq[2 * pp + 1]
            vn.append(u_st[c * npt + pp] - jnp.concatenate([w0[0:CHUNK], w1[0:CHUNK]], axis=0))
        oi = [_mm(in_st[c * npt + pp], vn[pp]) for pp in range(npt)]
        for pp in range(npt):
            idx = c * npt + pp
            bi, hp = divmod(pp, npair)
            ket = ket_st[idx]
            for i in range(2):
                hh = 2 * pp + i
                h = 2 * hp + i
                rs = slice(i * CHUNK, (i + 1) * CHUNK)
                keep = row_first if i == 0 else jnp.logical_not(row_first)
                state[hh] = (state[hh] * cd_st[idx, i * CHUNK:i * CHUNK + 1, :]
                             + _mm(ket, jnp.where(keep, vn[pp], 0.0)))
                o_h = wq[hh][CHUNK:c2] + oi[pp][rs]
                o_h = o_h * lax.rsqrt(jnp.mean(o_h * o_h, axis=-1, keepdims=True) + NORM_EPS) * nw_ref[...]
                for s in range(SUBLANES):
                    o_s[bi * nh + h, pl.ds(c * CHUNK + s, per, stride=SUBLANES), :] = o_h[s * per:(s + 1) * per]
        return carry

    lax.fori_loop(0, nc, chunk_body, 0)
    for bi in range(nbat):
        for h in range(nh):
            o_ref[bi, :, h * hd:(h + 1) * hd] = o_s[bi * nh + h].astype(o_ref.dtype)


def _gdn(qkv, ba, prm, bsz, seq, lb, inv_passes, nbat):
    bl, w3 = qkv.shape
    mixw = w3 // 3
    nh = mixw // GDN_HEAD
    per_b = seq // lb
    n = (lb // CHUNK) * nbat * (nh // 2)
    c2 = 2 * CHUNK
    full = lambda a: pl.BlockSpec(a.shape, lambda i, t: (0, 0))
    kern = functools.partial(_gdn_kernel, lb=lb, nh=nh, inv_passes=inv_passes)
    vm = lambda *s: pltpu.VMEM(s, F32)
    names = ['cw', 'alog', 'dtb', 'nw']
    out = pl.pallas_call(
        kern,
        grid=(bsz // nbat, per_b),
        in_specs=[pl.BlockSpec((nbat, lb, w3), lambda i, t: (i, t, 0)),
                  pl.BlockSpec((nbat, lb, LANES), lambda i, t: (i, t, 0))] + [full(prm[k]) for k in names],
        out_specs=pl.BlockSpec((nbat, lb, mixw), lambda i, t: (i, t, 0)),
        out_shape=jax.ShapeDtypeStruct((bsz, seq, mixw), BF16),
        scratch_shapes=[vm(nbat * 3 * nh, lb + SUBLANES, LANES), vm(nbat * nh, GDN_HEAD, GDN_HEAD)]
                       + [vm(n, c2, GDN_HEAD)] * 6
                       + [vm(n, GDN_HEAD, c2), vm(n, c2, c2), vm(n, c2, GDN_HEAD), vm(nbat * nh, lb, GDN_HEAD)],
        compiler_params=_cparams(("arbitrary", "arbitrary")),
        name="gdn",
    )(qkv.reshape(bsz, seq, w3), ba.reshape(bsz, seq, LANES), *[prm[k] for k in names])
    return out.reshape(bl, mixw)


def _finish(x_ref, g_ref, acc, fw_ref, o_ref, final):
    xn = x_ref[...] + g_ref[0] * acc
    if final:
        ms = jnp.mean(xn * xn, axis=-1, keepdims=True)
        xn = xn * lax.rsqrt(ms + NORM_EPS) * fw_ref[...]
    o_ref[...] = xn


def _outproj_even_kernel(conv_ref, u_ref, yb_ref, z_ref, x_ref, g_ref, d_ref, gw_ref, gb_ref, w_ref, fw_ref,
                         o_ref, cs_ref, us_ref, *, final):
    nj = conv_ref.shape[0]
    nrow = conv_ref.shape[2]
    for j in range(nj):
        for t in range(S5_T):
            cs_ref[j, pl.ds(t, nrow, stride=S5_T), :] = conv_ref[j, 0, :, t * LANES:(t + 1) * LANES].astype(F32)
            us_ref[j, pl.ds(t, nrow, stride=S5_T), :] = u_ref[j, 0, :, t * LANES:(t + 1) * LANES].astype(F32)
    ya = jnp.concatenate([cs_ref[j] for j in range(nj)], axis=1)
    uu = jnp.concatenate([us_ref[j] for j in range(nj)], axis=1)
    ya = _gelu_tanh(ya + d_ref[...] * uu)
    ya = ya * _sigmoid(_mm(ya, gw_ref[...]) + gb_ref[...])
    wa = ya.shape[1]
    sz = _silu(z_ref[...].astype(F32))
    acc = _mm(ya * sz[:, :wa], w_ref[0:wa, :]) + _mm(yb_ref[...].astype(F32) * sz[:, wa:], w_ref[wa:, :])
    _finish(x_ref, g_ref, acc, fw_ref, o_ref, final)


def _outproj_odd_kernel(y_ref, z_ref, x_ref, g_ref, w_ref, fw_ref, o_ref, *, final):
    acc = _mm(y_ref[...].astype(F32) * _silu(z_ref[...].astype(F32)), w_ref[...])
    _finish(x_ref, g_ref, acc, fw_ref, o_ref, final)


def kernel(x, c, norm_w, ada_w, ada_b, w_out, final_norm_w, even_w_in, s5_lambda_re, s5_lambda_im, s5_log_step, s5_b_re, s5_b_im, s5_c_re, s5_c_im, s5_d, s5_glu_w, s5_glu_b, rwkv_mu, rwkv_w0, rwkv_w_up, rwkv_a0, rwkv_a_up, rwkv_g_up, rwkv_k_k, rwkv_k_a, rwkv_r_k, rwkv_ln_w, rwkv_ln_b, odd_w_in, gdn_conv_w, gdn_a_log, gdn_dt_bias, gdn_norm_w):
    bsz, seq, d = x.shape
    depth = norm_w.shape[0]
    bl = bsz * seq
    tm = min(512, seq)
    lb = min(256, seq)
    inv_passes = 1
    nbat = 2 if bsz % 2 == 0 else 1
    s5w = s5_d.shape[1]
    nj = s5w // LANES
    rw = rwkv_w0.shape[1]
    mixw = w_out.shape[1]
    nh_gdn = mixw // GDN_HEAD
    per_b = seq // tm

    mod = _modulation(c, ada_w, ada_b)
    x2 = x.reshape(bl, d)
    row1 = lambda a: a.reshape(1, -1)
    tile_spec = lambda w: pl.BlockSpec((tm, w), lambda i: (i, 0))
    res_spec = lambda a: pl.BlockSpec(a.shape, lambda i: tuple(0 for _ in a.shape))
    gate_spec = pl.BlockSpec((1, 1, d), lambda i: (i // per_b, 0, 0))
    fw = row1(final_norm_w)

    for layer in range(depth):
        shift = mod[layer, :, 0:d].reshape(bsz, 1, d)
        scale = mod[layer, :, d:2 * d].reshape(bsz, 1, d)
        gate = mod[layer, :, 2 * d:3 * d].reshape(bsz, 1, d)
        i = layer // 2
        final = layer == depth - 1
        nw = row1(norm_w[layer])
        wo = w_out[layer].astype(BF16)
        if layer % 2 == 0:
            fwid = even_w_in.shape[2] - s5w - mixw
            s5_spec = pl.BlockSpec((nj, 1, tm // S5_T, S5_T * LANES), lambda r: (0, r // per_b, r % per_b, 0))
            s5_scratch = pltpu.VMEM((nj, tm, LANES), F32)
            u5, feats, z = _inproj(
                x2, nw, shift, scale, even_w_in[i].astype(BF16),
                functools.partial(_inproj_even_kernel, tn=512),
                (jax.ShapeDtypeStruct((nj, bsz, seq // S5_T, S5_T * LANES), BF16),
                 jax.ShapeDtypeStruct((bl, fwid), F32), jax.ShapeDtypeStruct((bl, mixw), BF16)),
                (s5_spec, tile_spec(fwid), tile_spec(mixw)),
                seq, tm, scratch=(s5_scratch,))
            conv5 = _s5_scan(u5, _s5_weights(s5_lambda_re[i], s5_lambda_im[i], s5_log_step[i], s5_b_re[i],
                                             s5_b_im[i], s5_c_re[i], s5_c_im[i]))
            zpad = jnp.zeros((LANES - DECAY_LORA, rw), F32)
            ones_bd = jnp.kron(jnp.eye(rw // RWKV_HEAD, dtype=F32), jnp.ones((RWKV_HEAD, RWKV_HEAD), F32))
            prm = dict(mu=row1(rwkv_mu[i]), w0=row1(rwkv_w0[i]),
                       wup=jnp.concatenate([rwkv_w_up[i], zpad], axis=0).astype(BF16),
                       a0=row1(rwkv_a0[i]),
                       aup=jnp.concatenate([zpad, rwkv_a_up[i]], axis=0).astype(BF16),
                       gup=rwkv_g_up[i].astype(BF16), kk=row1(rwkv_k_k[i]), ka=row1(rwkv_k_a[i]),
                       rk=row1(rwkv_r_k[i]), lnw=row1(rwkv_ln_w[i]), lnb=row1(rwkv_ln_b[i]),
                       ones=ones_bd.astype(BF16))
            yb = _rwkv(feats, prm, bsz, seq, lb, inv_passes, nbat)
            ins = (conv5, u5, yb, z, x2, gate, row1(s5_d[i]), s5_glu_w[i].astype(BF16), row1(s5_glu_b[i]), wo, fw)
            specs = [s5_spec, s5_spec,
                     tile_spec(rw), tile_spec(mixw), tile_spec(d), gate_spec] + [res_spec(a) for a in ins[6:]]
            kern = functools.partial(_outproj_even_kernel, final=final)
            out_scratch = [s5_scratch, s5_scratch]
        else:
            w_in = odd_w_in[i]
            q_end = 3 * mixw
            ba_w = jnp.concatenate([w_in[:, q_end:q_end + 2 * nh_gdn],
                                    jnp.zeros((d, LANES - 2 * nh_gdn), F32)], axis=1)
            w_r = jnp.concatenate([w_in[:, :q_end], w_in[:, q_end + 2 * nh_gdn:], ba_w], axis=1).astype(BF16)
            qkv, z, ba = _inproj(
                x2, nw, shift, scale, w_r,
                functools.partial(_inproj_odd_kernel, tn=512),
                (jax.ShapeDtypeStruct((bl, q_end), F32), jax.ShapeDtypeStruct((bl, mixw), BF16),
                 jax.ShapeDtypeStruct((bl, LANES), F32)),
                (tile_spec(q_end), tile_spec(mixw), tile_spec(LANES)),
                seq, tm)
            pad_row = lambda a: jnp.zeros((1, LANES), F32).at[0, nh_gdn:2 * nh_gdn].set(a)
            prm = dict(cw=gdn_conv_w[i], alog=pad_row(gdn_a_log[i]), dtb=pad_row(gdn_dt_bias[i]),
                       nw=row1(gdn_norm_w[i]))
            y = _gdn(qkv, ba, prm, bsz, seq, lb, inv_passes, nbat)
            ins = (y, z, x2, gate, wo, fw)
            specs = [tile_spec(mixw), tile_spec(mixw), tile_spec(d), gate_spec] + [res_spec(a) for a in ins[4:]]
            kern = functools.partial(_outproj_odd_kernel, final=final)
            out_scratch = []
        x2 = pl.pallas_call(
            kern,
            grid=(bl // tm,),
            in_specs=specs,
            out_specs=tile_spec(d),
            out_shape=jax.ShapeDtypeStruct((bl, d), F32),
            scratch_shapes=out_scratch,
            compiler_params=_cparams(("arbitrary",)),
            name="out_proj",
        )(*ins)
    return x2.reshape(bsz, seq, d)
```

```python
import functools
import math

import jax
import jax.numpy as jnp
from jax import lax
from jax.experimental import pallas as pl
from jax.experimental.pallas import tpu as pltpu

F32 = jnp.float32
BF16 = jnp.bfloat16

NORM_EPS = 1e-6
LANES = 128
SUBLANES = 8
VMEM_LIMIT = 56 * 1024 * 1024

S5_GROUP = 16
S5_STATE = 64
RWKV_HEAD = 64
DECAY_LORA = 64
ICLR_LORA = 64
GATE_LORA = 128
RWKV_DECAY_SCALE = math.exp(-0.5)
GDN_HEAD = 128
GDN_CONV = 4
CHUNK = 64
S5_T = 8

NN = (((1,), (0,)), ((), ()))
NT = (((1,), (1,)), ((), ()))
BNN = (((2,), (1,)), ((0,), (0,)))
BNT = (((2,), (2,)), ((0,), (0,)))


def _split(a, n):
    if a.dtype == BF16:
        return [a]
    parts = []
    r = a
    for i in range(n):
        p = r.astype(BF16)
        parts.append(p)
        if i + 1 < n:
            r = r - p.astype(F32)
    return parts


def _mm(a, b, dn=NN, pa=1, pb=1):
    ap = _split(a, pa)
    bp = _split(b, pb)
    lim = max(len(ap), len(bp))
    acc = None
    for i, x in enumerate(ap):
        for j, y in enumerate(bp):
            if i + j >= lim:
                continue
            t = lax.dot_general(x, y, dn, preferred_element_type=F32)
            acc = t if acc is None else acc + t
    return acc


def _sigmoid(x):
    return 0.5 + 0.5 * jnp.tanh(0.5 * x)


def _silu(x):
    h = 0.5 * x
    return h + h * jnp.tanh(h)


def _softplus(x):
    return jnp.maximum(x, 0.0) + jnp.log(1.0 + jnp.exp(-jnp.abs(x)))


def _gelu_tanh(x):
    c = math.sqrt(2.0 / math.pi)
    return 0.5 * x * (1.0 + jnp.tanh(c * (x + 0.044715 * (x * x * x))))


def _iota2(shape, dim):
    return lax.broadcasted_iota(jnp.int32, shape, dim)


def _inv_unit_lower(n_mat, eye, passes, dn, row, col):
    t = eye + jnp.where((row // 2) == (col // 2), n_mat, 0.0)
    b = 2
    while b < CHUNK:
        off = ((row // (2 * b)) == (col // (2 * b))) & ((row // b) % 2 == 1) & ((col // b) % 2 == 0)
        x = _mm(jnp.where(off, n_mat, 0.0), t, dn, pa=passes, pb=passes)
        t = t + _mm(t, x, dn, pa=passes, pb=passes)
        b *= 2
    return t


def _cparams(sem):
    return pltpu.CompilerParams(dimension_semantics=sem, vmem_limit_bytes=VMEM_LIMIT)


def _mod_kernel(c_ref, w_ref, b_ref, o_ref):
    s = _silu(c_ref[...])
    o_ref[0] = _mm(s, w_ref[0], pa=2, pb=2) + b_ref[0]


def _modulation(c, ada_w, ada_b):
    depth, d, d3 = ada_w.shape
    bsz = c.shape[0]
    nj = d3 // d
    return pl.pallas_call(
        _mod_kernel,
        grid=(depth, nj),
        in_specs=[
            pl.BlockSpec((bsz, d), lambda l, j: (0, 0)),
            pl.BlockSpec((1, d, d), lambda l, j: (l, 0, j)),
            pl.BlockSpec((1, 1, d), lambda l, j: (l, 0, j)),
        ],
        out_specs=pl.BlockSpec((1, bsz, d), lambda l, j: (l, 0, j)),
        out_shape=jax.ShapeDtypeStruct((depth, bsz, d3), F32),
        compiler_params=_cparams(("arbitrary", "arbitrary")),
        name="adaln_mod",
    )(c, ada_w, ada_b.reshape(depth, 1, d3))


def _norm_mod(x, nw, scale, shift):
    ms = jnp.mean(x * x, axis=-1, keepdims=True)
    h = x * lax.rsqrt(ms + NORM_EPS) * nw
    return h * (1.0 + scale) + shift


def _inproj_even_kernel(x_ref, nw_ref, sh_ref, sc_ref, w_ref, u_ref, f_ref, z_ref, us_ref, *, tn):
    hb = _norm_mod(x_ref[...], nw_ref[...], sc_ref[0], sh_ref[0]).astype(BF16)
    nu = u_ref.shape[0]
    nrow = u_ref.shape[2]
    for j in range(nu):
        us_ref[j] = _mm(hb, w_ref[:, j * LANES:(j + 1) * LANES])
        for t in range(S5_T):
            u_ref[j, 0, :, t * LANES:(t + 1) * LANES] = (
                us_ref[j, pl.ds(t, nrow, stride=S5_T), :].astype(u_ref.dtype))
    off = nu * LANES
    for o_ref in (f_ref, z_ref):
        width = o_ref.shape[1]
        for j0 in range(0, width, tn):
            w = min(tn, width - j0)
            o_ref[:, j0:j0 + w] = _mm(hb, w_ref[:, off + j0:off + j0 + w]).astype(o_ref.dtype)
        off += width


def _inproj_odd_kernel(x_ref, nw_ref, sh_ref, sc_ref, w_ref, qkv_ref, z_ref, ba_ref, *, tn):
    hb = _norm_mod(x_ref[...], nw_ref[...], sc_ref[0], sh_ref[0]).astype(BF16)
    off = 0
    for o_ref in (qkv_ref, z_ref, ba_ref):
        width = o_ref.shape[1]
        for j0 in range(0, width, tn):
            w = min(tn, width - j0)
            o_ref[:, j0:j0 + w] = _mm(hb, w_ref[:, off + j0:off + j0 + w]).astype(o_ref.dtype)
        off += width


def _inproj(x2, nw, shift, scale, w_bf, kernel_fn, out_shapes, out_specs, seq, tm, scratch=(), extra=()):
    bl, d = x2.shape
    n = w_bf.shape[1]
    per_b = seq // tm
    return pl.pallas_call(
        kernel_fn,
        grid=(bl // tm,),
        in_specs=[
            pl.BlockSpec((tm, d), lambda i: (i, 0)),
            pl.BlockSpec((1, d), lambda i: (0, 0)),
            pl.BlockSpec((1, 1, d), lambda i: (i // per_b, 0, 0)),
            pl.BlockSpec((1, 1, d), lambda i: (i // per_b, 0, 0)),
            pl.BlockSpec((d, n), lambda i: (0, 0)),
        ] + [pl.BlockSpec(a.shape, lambda i: (0, 0)) for a in extra],
        out_specs=out_specs,
        out_shape=out_shapes,
        scratch_shapes=list(scratch),
        compiler_params=_cparams(("arbitrary",)),
        name="in_proj",
    )(x2, nw, shift, scale, w_bf, *extra)


def _s5_kernel(u_ref, kd_ref, p_ref, q_ref, are_ref, aim_ref, y_ref, st_ref, sloc_ref, sprev_ref, m_ref, *, tc, nb):
    cb = pl.program_id(1)

    @pl.when(cb == 0)
    def _():
        st_ref[...] = jnp.zeros_like(st_ref)
        for s in range(S5_T):
            for t in range(S5_T):
                blk = kd_ref[t - s, 0].astype(BF16) if t >= s else jnp.zeros((LANES, LANES), BF16)
                m_ref[s * LANES:(s + 1) * LANES, t * LANES:(t + 1) * LANES] = blk

    kcols = u_ref.shape[-1]
    x = u_ref[0].reshape(nb * tc, kcols).astype(BF16)
    y = _mm(x, m_ref[...])
    sloc = _mm(x, p_ref[0])
    nt2 = sloc_ref.shape[0]
    nt = nt2 // 2
    pitch = tc + SUBLANES
    for k in range(nt2):
        for b in range(nb):
            sloc_ref[k, b * pitch:b * pitch + tc, :] = sloc[b * tc:(b + 1) * tc, k * LANES:(k + 1) * LANES]
    a_re = are_ref[0]
    a_im = aim_ref[0]

    def body(c, carry):
        rows = pl.ds(c, nb, stride=pitch)
        new = []
        for k in range(nt):
            s_re, s_im = carry[k], carry[nt + k]
            sprev_ref[k, rows, :] = s_re
            sprev_ref[nt + k, rows, :] = s_im
            ar = a_re[:, k * LANES:(k + 1) * LANES]
            ai = a_im[:, k * LANES:(k + 1) * LANES]
            new.append((ar * s_re - ai * s_im + sloc_ref[k, rows, :],
                        ar * s_im + ai * s_re + sloc_ref[nt + k, rows, :]))
        return tuple(n[0] for n in new) + tuple(n[1] for n in new)

    st = lax.fori_loop(0, tc, body, tuple(st_ref[k] for k in range(nt2)))
    for k in range(nt2):
        st_ref[k] = st[k]
    sprev = jnp.concatenate(
        [jnp.concatenate([sprev_ref[k, b * pitch:b * pitch + tc, :] for b in range(nb)], axis=0)
         for k in range(nt2)], axis=1)
    y = y + _mm(sprev, q_ref[0])
    y_ref[0] = y.reshape(nb, tc, kcols).astype(y_ref.dtype)


def _s5_weights(lambda_re, lambda_im, log_step, b_re, b_im, c_re, c_im):
    g_all, p_all = lambda_re.shape
    t_len = S5_T
    gpt = LANES // S5_GROUP
    nj = g_all // gpt
    lam = lax.complex(jnp.minimum(lambda_re, -1e-4), lambda_im)
    step = jnp.exp(log_step)[:, None]
    lam_bar = jnp.exp(lam * step)
    b_bar = ((lam_bar - 1.0) / lam)[..., None] * lax.complex(b_re, b_im)
    c_mat = lax.complex(c_re, c_im)
    taus = jnp.arange(t_len + 1, dtype=F32)
    pw = jnp.exp((lam * step)[None] * taus[:, None, None])
    hi = lax.Precision.HIGHEST
    kern = jnp.real(jnp.einsum('gcp,tgp,gpd->tgcd', c_mat, pw[:t_len], b_bar, precision=hi))
    s_idx = jnp.arange(t_len)
    eye_g = jnp.eye(gpt, dtype=F32)
    kd = jnp.einsum('tjgoi,hg->tjhigo', kern.reshape(t_len, nj, gpt, S5_GROUP, S5_GROUP), eye_g)
    kd = kd.reshape(t_len, nj, LANES, LANES)
    pb = pw[t_len - 1 - s_idx][:, :, :, None] * b_bar[None]
    pb = jnp.stack([jnp.real(pb), jnp.imag(pb)], axis=0)
    pb = pb.reshape(2, t_len, nj, gpt, p_all, S5_GROUP)
    p_mat = jnp.einsum('rsjgpi,hg->jshirgp', pb, eye_g).astype(BF16).reshape(nj, t_len * LANES, 2 * gpt * p_all)
    cq = c_mat[None] * pw[1:t_len + 1][:, :, None, :]
    cq = jnp.stack([jnp.real(cq), -jnp.imag(cq)], axis=0)
    cq = cq.reshape(2, t_len, nj, gpt, S5_GROUP, p_all)
    q_mat = jnp.einsum('rtjgop,hg->jrhptgo', cq, eye_g).astype(BF16).reshape(nj, 2 * gpt * p_all, t_len * LANES)
    a_t = pw[t_len].reshape(nj, 1, gpt * p_all)
    return (kd, p_mat, q_mat,
            jnp.real(a_t).astype(F32), jnp.imag(a_t).astype(F32))


def _s5_scan(u5, weights):
    kd, p_mat, q_mat, a_re, a_im = weights
    nj, bsz, nchunks, kcols = u5.shape
    tc = min(64, nchunks)
    ns2 = p_mat.shape[-1]
    kern = functools.partial(_s5_kernel, tc=tc, nb=bsz)
    return pl.pallas_call(
        kern,
        grid=(nj, nchunks // tc),
        in_specs=[
            pl.BlockSpec((1, bsz, tc, kcols), lambda j, c: (j, 0, c, 0)),
            pl.BlockSpec((S5_T, 1, LANES, LANES), lambda j, c: (0, j, 0, 0)),
            pl.BlockSpec((1, kcols, ns2), lambda j, c: (j, 0, 0)),
            pl.BlockSpec((1, ns2, kcols), lambda j, c: (j, 0, 0)),
            pl.BlockSpec((1, 1, ns2 // 2), lambda j, c: (j, 0, 0)),
            pl.BlockSpec((1, 1, ns2 // 2), lambda j, c: (j, 0, 0)),
        ],
        out_specs=pl.BlockSpec((1, bsz, tc, kcols), lambda j, c: (j, 0, c, 0)),
        out_shape=jax.ShapeDtypeStruct((nj, bsz, nchunks, kcols), BF16),
        scratch_shapes=[
            pltpu.VMEM((ns2 // LANES, bsz, LANES), F32),
            pltpu.VMEM((ns2 // LANES, bsz * (tc + SUBLANES), LANES), F32),
            pltpu.VMEM((ns2 // LANES, bsz * (tc + SUBLANES), LANES), F32),
            pltpu.VMEM((kcols, kcols), BF16),
        ],
        compiler_params=_cparams(("arbitrary", "arbitrary")),
        name="s5_scan",
    )(u5, kd, p_mat, q_mat, a_re, a_im)


def _rwkv_kernel(f_ref, mu_ref, w0_ref, wup_ref, a0_ref, aup_ref, gup_ref, kk_ref, ka_ref, rk_ref,
                 lnw_ref, lnb_ref, ones_ref, o_ref,
                 fbuf, state, a_st, r_st, b_st, k_st, v_st, arw_st, tav_st, rkv_st, arb_st, bkd_st, vk_st,
                 ee_st, o_s, *, lb, inv_passes):
    t = pl.program_id(1)
    nbat = f_ref.shape[0]
    width = o_s.shape[1]
    npair = width // LANES
    nchain = nbat * npair
    nc = lb // CHUNK
    c2 = 2 * CHUNK

    @pl.when(t == 0)
    def _():
        fbuf[:, 0:SUBLANES, :] = jnp.zeros((nbat, SUBLANES, fbuf.shape[2]), F32)
        state[...] = jnp.zeros_like(state)

    @pl.when(t > 0)
    def _():
        fbuf[:, 0:SUBLANES, :] = fbuf[:, lb:lb + SUBLANES, :]

    xl, xsl = [], []
    for bi in range(nbat):
        xb = f_ref[bi]
        fbuf[bi, SUBLANES:SUBLANES + lb, :] = xb
        xl.append(xb)
        xsl.append(fbuf[bi, pl.ds(SUBLANES - 1, lb), :])
    x = jnp.concatenate(xl, axis=0)
    xs = jnp.concatenate(xsl, axis=0)
    f = x + mu_ref[...] * (xs - x)
    r = f[:, 0:width]
    k = f[:, width:2 * width]
    v = f[:, 2 * width:3 * width]
    lora = f[:, 3 * width:3 * width + LANES]
    xg = f[:, 3 * width + LANES:3 * width + 2 * LANES]
    lw = -RWKV_DECAY_SCALE * _sigmoid(w0_ref[...] + _mm(jnp.tanh(lora), wup_ref[...]))
    a = _sigmoid(a0_ref[...] + _mm(lora, aup_ref[...]))
    g = _mm(_sigmoid(xg), gup_ref[...])
    half = width // 2
    ones_half = ones_ref[0:half, 0:half]

    def head_sum(m):
        return jnp.concatenate([_mm(m[:, 0:half], ones_half), _mm(m[:, half:width], ones_half)], axis=1)

    kk = k * kk_ref[...]
    kn = kk * lax.rsqrt(head_sum(kk * kk) + 1e-6)
    kp = k * (1.0 + (a - 1.0) * ka_ref[...])
    bonus = head_sum(r * kp * rk_ref[...]) * v
    bb = kn * a

    row = _iota2((c2, c2), 0)
    col = _iota2((c2, c2), 1)
    same = (row // CHUNK) == (col // CHUNK)
    strict = same & (row > col)
    incl = same & (row >= col)
    eye = jnp.where(row == col, 1.0, 0.0).astype(F32)
    tri = jnp.where(_iota2((CHUNK, CHUNK), 0) >= _iota2((CHUNK, CHUNK), 1), 1.0, 0.0).astype(BF16)
    first = _iota2((CHUNK, LANES), 1) < RWKV_HEAD

    def stack(m):
        return jnp.concatenate([jnp.where(first, m, 0.0), jnp.where(first, 0.0, m)], axis=0)

    for c in range(nc):
        for bi in range(nbat):
            cs = slice(bi * lb + c * CHUNK, bi * lb + (c + 1) * CHUNK)
            lwc = lw[cs]
            gc = _mm(tri, lwc, pb=3)
            e_g = jnp.exp(gc)
            e_ng = jnp.exp(-gc)
            at = -kn[cs] * jnp.exp(gc - lwc)
            bt = bb[cs] * e_ng
            kt = kp[cs] * e_ng
            rt = r[cs] * e_g
            vv = v[cs]
            e_end = jnp.broadcast_to(jnp.exp(gc[CHUNK - 1:CHUNK, :]), (SUBLANES, width))
            for p in range(npair):
                idx = c * nchain + bi * npair + p
                sl = slice(p * LANES, (p + 1) * LANES)
                a_st[idx] = stack(at[:, sl])
                r_st[idx] = stack(rt[:, sl])
                b_st[idx] = stack(bt[:, sl])
                k_st[idx] = stack(kt[:, sl])
                v_st[idx] = stack(vv[:, sl])
                ee_st[idx] = e_end[:, sl]

    a_all = a_st[...]
    r_all = r_st[...]
    v_all = v_st[...]
    ar = jnp.concatenate([a_all, r_all], axis=1)
    bk = jnp.concatenate([b_st[...], k_st[...]], axis=1)
    p4 = _mm(ar, bk, BNT)
    a_ab = jnp.where(strict, p4[:, 0:c2, 0:c2], 0.0)
    a_ak = jnp.where(strict, p4[:, 0:c2, c2:2 * c2], 0.0)
    a_rb = jnp.where(incl, p4[:, c2:2 * c2, 0:c2], 0.0)
    a_rk = jnp.where(incl, p4[:, c2:2 * c2, c2:2 * c2], 0.0)
    t_inv = _inv_unit_lower(a_ab, eye, inv_passes, BNN, row, col)
    tt = _mm(t_inv, jnp.concatenate([a_all, _mm(a_ak, v_all, BNN)], axis=2), BNN, pa=inv_passes)
    arw_st[...] = jnp.concatenate([tt[:, :, 0:LANES], r_all], axis=1)
    tav_st[...] = tt[:, :, LANES:2 * LANES]
    rkv_st[...] = _mm(a_rk, v_all, BNN)
    arb_st[...] = a_rb
    bkd = bk * ee_st[:, 0:1, :]
    bkd_st[...] = bkd[:, 0:c2, :]
    vk_st[...] = _mm(jnp.swapaxes(v_all, 1, 2), bkd[:, c2:2 * c2, :], BNN)

    def chunk_body(c, carry):
        xr = [_mm(arw_st[c * nchain + q], state[q], NT) for q in range(nchain)]
        uu = [xr[q][0:c2] + tav_st[c * nchain + q] for q in range(nchain)]
        ob = [_mm(arb_st[c * nchain + q], uu[q]) for q in range(nchain)]
        for q in range(nchain):
            bi, p = divmod(q, npair)
            idx = c * nchain + q
            rows = pl.ds(pl.multiple_of(bi * lb + c * CHUNK, CHUNK), CHUNK)
            o_st = xr[q][c2:2 * c2] + ob[q] + rkv_st[idx]
            o_s[rows, p * LANES:(p + 1) * LANES] = o_st[0:CHUNK] + o_st[CHUNK:c2]
            state[q] = state[q] * ee_st[idx, 0:1, :] + vk_st[idx] + _mm(uu[q].T, bkd_st[idx])
        return carry

    lax.fori_loop(0, nc, chunk_body, 0)

    o = o_s[...]
    inv_n = 1.0 / RWKV_HEAD
    mean = head_sum(o) * inv_n
    d = o - mean
    var = head_sum(d * d) * inv_n
    on = d * lax.rsqrt(var + 1e-5 * RWKV_HEAD) * lnw_ref[...] + lnb_ref[...]
    out = ((on + bonus) * g).astype(o_ref.dtype)
    for bi in range(nbat):
        o_ref[bi] = out[bi * lb:(bi + 1) * lb]


def _rwkv(feats, prm, bsz, seq, lb, inv_passes, nbat):
    bl, fw = feats.shape
    width = prm['w0'].shape[1]
    full = lambda a: pl.BlockSpec(a.shape, lambda i, t: (0, 0))
    names = ['mu', 'w0', 'wup', 'a0', 'aup', 'gup', 'kk', 'ka', 'rk', 'lnw', 'lnb', 'ones']
    per_b = seq // lb
    nchain = nbat * (width // LANES)
    n = (lb // CHUNK) * nchain
    c2 = 2 * CHUNK
    kern = functools.partial(_rwkv_kernel, lb=lb, inv_passes=inv_passes)
    vm = lambda *s: pltpu.VMEM(s, F32)
    out = pl.pallas_call(
        kern,
        grid=(bsz // nbat, per_b),
        in_specs=[pl.BlockSpec((nbat, lb, fw), lambda i, t: (i, t, 0))] + [full(prm[k]) for k in names],
        out_specs=pl.BlockSpec((nbat, lb, width), lambda i, t: (i, t, 0)),
        out_shape=jax.ShapeDtypeStruct((bsz, seq, width), BF16),
        scratch_shapes=[vm(nbat, lb + SUBLANES, fw), vm(nchain, LANES, LANES)]
                       + [vm(n, c2, LANES)] * 5
                       + [vm(n, 2 * c2, LANES), vm(n, c2, LANES), vm(n, c2, LANES), vm(n, c2, c2),
                          vm(n, c2, LANES), vm(n, LANES, LANES), vm(n, SUBLANES, LANES), vm(nbat * lb, width)],
        compiler_params=_cparams(("arbitrary", "arbitrary")),
        name="rwkv7",
    )(feats.reshape(bsz, seq, fw), *[prm[k] for k in names])
    return out.reshape(bl, width)


def _gdn_kernel(qkv_ref, ba_ref, cw_ref, alog_ref, dtb_ref, nw_ref, o_ref,
                cbuf, state, q_st, k_st, v_st, u_st, w_st, qd_st, ket_st, in_st, cd_st, o_s, *, lb, nh, inv_passes):
    t = pl.program_id(1)
    nbat = qkv_ref.shape[0]
    hd = GDN_HEAD
    nc = lb // CHUNK
    npair = nh // 2
    npt = nbat * npair
    c2 = 2 * CHUNK
    ntile = 3 * nh
    nrow = lb // SUBLANES
    per = CHUNK // SUBLANES

    @pl.when(t == 0)
    def _():
        cbuf[:, 0:SUBLANES, :] = jnp.zeros((nbat * ntile, SUBLANES, LANES), F32)
        state[...] = jnp.zeros_like(state)

    @pl.when(t > 0)
    def _():
        cbuf[:, 0:SUBLANES, :] = cbuf[:, lb:lb + SUBLANES, :]

    for bi in range(nbat):
        for lt in range(ntile):
            ct = bi * ntile + lt
            cbuf[ct, SUBLANES:SUBLANES + lb, :] = qkv_ref[bi, :, lt * LANES:(lt + 1) * LANES]
            slab = {s: cbuf[ct, pl.ds(SUBLANES + s, nrow, stride=SUBLANES), :]
                    for s in range(1 - GDN_CONV, SUBLANES)}
            cwt = [cw_ref[j:j + 1, lt * LANES:(lt + 1) * LANES] for j in range(GDN_CONV)]
            kind, h = divmod(lt, nh)
            dst = (q_st, k_st, v_st)[kind]
            hp, i = divmod(h, 2)
            for s in range(SUBLANES):
                acc = cwt[0] * slab[s + 1 - GDN_CONV]
                for j in range(1, GDN_CONV):
                    acc = acc + cwt[j] * slab[s + 1 - GDN_CONV + j]
                y = _silu(acc)
                if kind == 0:
                    y = y * (lax.rsqrt(jnp.sum(y * y, axis=-1, keepdims=True) + 1e-6) * (hd ** -0.5))
                elif kind == 1:
                    y = y * lax.rsqrt(jnp.sum(y * y, axis=-1, keepdims=True) + 1e-6)
                for c in range(nc):
                    r0 = i * CHUNK + s * per
                    dst[c * npt + bi * npair + hp, r0:r0 + per, :] = y[c * per:(c + 1) * per]

    def time_of(idx):
        within = idx % CHUNK
        return (idx // CHUNK) * CHUNK + (within % per) * SUBLANES + within // per

    row = time_of(_iota2((c2, c2), 0))
    col = time_of(_iota2((c2, c2), 1))
    same = (row // CHUNK) == (col // CHUNK)
    strict = same & (row > col)
    incl = same & (row >= col)
    eye = jnp.where(row == col, 1.0, 0.0).astype(F32)
    tri = jnp.where(time_of(_iota2((CHUNK, CHUNK), 0)) >= time_of(_iota2((CHUNK, CHUNK), 1)), 1.0, 0.0).astype(BF16)
    lane_first = _iota2((1, c2), 1) < CHUNK
    row_first = _iota2((c2, hd), 0) < CHUNK

    gcols, grows, bcols, gends = [], [], [], []
    for c in range(nc):
        for bi in range(nbat):
            ba = jnp.concatenate([ba_ref[bi, pl.ds(c * CHUNK + s, per, stride=SUBLANES), :]
                                  for s in range(SUBLANES)], axis=0)
            bc = _sigmoid(ba)
            ld = -jnp.exp(alog_ref[...]) * _softplus(ba + dtb_ref[...])
            g = _mm(tri, ld, pb=3)
            g2t = jnp.concatenate([g, g], axis=0).T
            for hp in range(npair):
                l0 = nh + 2 * hp
                l1 = l0 + 1
                gcols.append(jnp.concatenate([g[:, l0:l0 + 1], g[:, l1:l1 + 1]], axis=0))
                grows.append(jnp.where(lane_first, g2t[l0:l0 + 1, :], g2t[l1:l1 + 1, :]))
                bcols.append(jnp.concatenate([bc[:, 2 * hp:2 * hp + 1], bc[:, 2 * hp + 1:2 * hp + 2]], axis=0))
                gends.append(jnp.concatenate([jnp.broadcast_to(g[CHUNK - 1:CHUNK, l0:l0 + 1], (CHUNK, 1)),
                                              jnp.broadcast_to(g[CHUNK - 1:CHUNK, l1:l1 + 1], (CHUNK, 1))], axis=0))
    gcol = jnp.stack(gcols)
    grow = jnp.stack(grows)
    bcol = jnp.stack(bcols)
    gend = jnp.stack(gends)

    q_all = q_st[...]
    k_all = k_st[...]
    v_all = v_st[...]
    dec = jnp.where(incl, jnp.exp(jnp.where(incl, gcol - grow, 0.0)), 0.0)
    e_g = jnp.exp(gcol)
    kb = k_all * bcol
    a_mat = jnp.where(strict, _mm(kb, k_all, BNT) * dec, 0.0)
    in_st[...] = jnp.where(incl, _mm(q_all, k_all, BNT) * dec, 0.0)
    t_inv = _inv_unit_lower(-a_mat, eye, inv_passes, BNN, row, col)
    uw = _mm(t_inv, jnp.concatenate([v_all * bcol, kb * e_g], axis=2), BNN, pa=inv_passes)
    u_st[...] = uw[:, :, 0:hd]
    w_st[...] = uw[:, :, hd:2 * hd]
    qd_st[...] = q_all * e_g
    ket_st[...] = jnp.swapaxes(k_all * jnp.exp(gend - gcol), 1, 2)
    cd_st[...] = jnp.broadcast_to(jnp.exp(gend), cd_st.shape)

    def chunk_body(c, carry):
        wq = []
        for hh in range(nbat * nh):
            pp, i = divmod(hh, 2)
            rs = slice(i * CHUNK, (i + 1) * CHUNK)
            lhs = jnp.concatenate([w_st[c * npt + pp, rs, :], qd_st[c * npt + pp, rs, :]], axis=0)
            wq.append(_mm(lhs, state[hh]))
        vn = []
        for pp in range(npt):
            w0 = wq[2 * pp]
            w1 = wq[2 * pp + 1]
            vn.append(u_st[c * npt + pp] - jnp.concatenate([w0[0:CHUNK], w1[0:CHUNK]], axis=0))
        oi = [_mm(in_st[c * npt + pp], vn[pp]) for pp in range(npt)]
        for pp in range(npt):
            idx = c * npt + pp
            bi, hp = divmod(pp, npair)
            kv = _mm(ket_st[idx], jnp.concatenate([jnp.where(row_first, vn[pp], 0.0),
                                                   jnp.where(row_first, 0.0, vn[pp])], axis=1))
            for i in range(2):
                hh = 2 * pp + i
                h = 2 * hp + i
                rs = slice(i * CHUNK, (i + 1) * CHUNK)
                state[hh] = state[hh] * cd_st[idx, i * CHUNK:i * CHUNK + 1, :] + kv[:, i * hd:(i + 1) * hd]
                o_h = wq[hh][CHUNK:c2] + oi[pp][rs]
                o_h = o_h * lax.rsqrt(jnp.mean(o_h * o_h, axis=-1, keepdims=True) + NORM_EPS) * nw_ref[...]
                for s in range(SUBLANES):
                    o_s[bi * nh + h, pl.ds(c * CHUNK + s, per, stride=SUBLANES), :] = o_h[s * per:(s + 1) * per]
        return carry

    lax.fori_loop(0, nc, chunk_body, 0)
    for bi in range(nbat):
        for h in range(nh):
            o_ref[bi, :, h * hd:(h + 1) * hd] = o_s[bi * nh + h].astype(o_ref.dtype)


def _gdn(qkv, ba, prm, bsz, seq, lb, inv_passes, nbat):
    bl, w3 = qkv.shape
    mixw = w3 // 3
    nh = mixw // GDN_HEAD
    per_b = seq // lb
    n = (lb // CHUNK) * nbat * (nh // 2)
    c2 = 2 * CHUNK
    full = lambda a: pl.BlockSpec(a.shape, lambda i, t: (0, 0))
    kern = functools.partial(_gdn_kernel, lb=lb, nh=nh, inv_passes=inv_passes)
    vm = lambda *s: pltpu.VMEM(s, F32)
    names = ['cw', 'alog', 'dtb', 'nw']
    out = pl.pallas_call(
        kern,
        grid=(bsz // nbat, per_b),
        in_specs=[pl.BlockSpec((nbat, lb, w3), lambda i, t: (i, t, 0)),
                  pl.BlockSpec((nbat, lb, LANES), lambda i, t: (i, t, 0))] + [full(prm[k]) for k in names],
        out_specs=pl.BlockSpec((nbat, lb, mixw), lambda i, t: (i, t, 0)),
        out_shape=jax.ShapeDtypeStruct((bsz, seq, mixw), BF16),
        scratch_shapes=[vm(nbat * 3 * nh, lb + SUBLANES, LANES), vm(nbat * nh, GDN_HEAD, GDN_HEAD)]
                       + [vm(n, c2, GDN_HEAD)] * 6
                       + [vm(n, GDN_HEAD, c2), vm(n, c2, c2), vm(n, c2, GDN_HEAD), vm(nbat * nh, lb, GDN_HEAD)],
        compiler_params=_cparams(("arbitrary", "arbitrary")),
        name="gdn",
    )(qkv.reshape(bsz, seq, w3), ba.reshape(bsz, seq, LANES), *[prm[k] for k in names])
    return out.reshape(bl, mixw)


def _finish(x_ref, g_ref, acc, fw_ref, o_ref, final):
    xn = x_ref[...] + g_ref[0] * acc
    if final:
        ms = jnp.mean(xn * xn, axis=-1, keepdims=True)
        xn = xn * lax.rsqrt(ms + NORM_EPS) * fw_ref[...]
    o_ref[...] = xn


def _outproj_even_kernel(conv_ref, u_ref, yb_ref, z_ref, x_ref, g_ref, d_ref, gw_ref, gb_ref, w_ref, fw_ref,
                         o_ref, cs_ref, us_ref, *, final):
    nj = conv_ref.shape[0]
    nrow = conv_ref.shape[2]
    for j in range(nj):
        for t in range(S5_T):
            cs_ref[j, pl.ds(t, nrow, stride=S5_T), :] = conv_ref[j, 0, :, t * LANES:(t + 1) * LANES].astype(F32)
            us_ref[j, pl.ds(t, nrow, stride=S5_T), :] = u_ref[j, 0, :, t * LANES:(t + 1) * LANES].astype(F32)
    ya = jnp.concatenate([cs_ref[j] for j in range(nj)], axis=1)
    uu = jnp.concatenate([us_ref[j] for j in range(nj)], axis=1)
    ya = _gelu_tanh(ya + d_ref[...] * uu)
    ya = ya * _sigmoid(_mm(ya, gw_ref[...]) + gb_ref[...])
    wa = ya.shape[1]
    sz = _silu(z_ref[...].astype(F32))
    acc = _mm(ya * sz[:, :wa], w_ref[0:wa, :]) + _mm(yb_ref[...].astype(F32) * sz[:, wa:], w_ref[wa:, :])
    _finish(x_ref, g_ref, acc, fw_ref, o_ref, final)


def _outproj_odd_kernel(y_ref, z_ref, x_ref, g_ref, w_ref, fw_ref, o_ref, *, final):
    acc = _mm(y_ref[...].astype(F32) * _silu(z_ref[...].astype(F32)), w_ref[...])
    _finish(x_ref, g_ref, acc, fw_ref, o_ref, final)


def kernel(x, c, norm_w, ada_w, ada_b, w_out, final_norm_w, even_w_in, s5_lambda_re, s5_lambda_im, s5_log_step, s5_b_re, s5_b_im, s5_c_re, s5_c_im, s5_d, s5_glu_w, s5_glu_b, rwkv_mu, rwkv_w0, rwkv_w_up, rwkv_a0, rwkv_a_up, rwkv_g_up, rwkv_k_k, rwkv_k_a, rwkv_r_k, rwkv_ln_w, rwkv_ln_b, odd_w_in, gdn_conv_w, gdn_a_log, gdn_dt_bias, gdn_norm_w):
    bsz, seq, d = x.shape
    depth = norm_w.shape[0]
    bl = bsz * seq
    tm = min(512, seq)
    lb = min(256, seq)
    inv_passes = 1
    nbat = 2 if bsz % 2 == 0 else 1
    s5w = s5_d.shape[1]
    nj = s5w // LANES
    rw = rwkv_w0.shape[1]
    mixw = w_out.shape[1]
    nh_gdn = mixw // GDN_HEAD
    per_b = seq // tm

    mod = _modulation(c, ada_w, ada_b)
    x2 = x.reshape(bl, d)
    row1 = lambda a: a.reshape(1, -1)
    tile_spec = lambda w: pl.BlockSpec((tm, w), lambda i: (i, 0))
    res_spec = lambda a: pl.BlockSpec(a.shape, lambda i: tuple(0 for _ in a.shape))
    gate_spec = pl.BlockSpec((1, 1, d), lambda i: (i // per_b, 0, 0))
    fw = row1(final_norm_w)

    for layer in range(depth):
        shift = mod[layer, :, 0:d].reshape(bsz, 1, d)
        scale = mod[layer, :, d:2 * d].reshape(bsz, 1, d)
        gate = mod[layer, :, 2 * d:3 * d].reshape(bsz, 1, d)
        i = layer // 2
        final = layer == depth - 1
        nw = row1(norm_w[layer])
        wo = w_out[layer].astype(BF16)
        if layer % 2 == 0:
            fwid = even_w_in.shape[2] - s5w - mixw
            s5_spec = pl.BlockSpec((nj, 1, tm // S5_T, S5_T * LANES), lambda r: (0, r // per_b, r % per_b, 0))
            s5_scratch = pltpu.VMEM((nj, tm, LANES), F32)
            u5, feats, z = _inproj(
                x2, nw, shift, scale, even_w_in[i].astype(BF16),
                functools.partial(_inproj_even_kernel, tn=512),
                (jax.ShapeDtypeStruct((nj, bsz, seq // S5_T, S5_T * LANES), BF16),
                 jax.ShapeDtypeStruct((bl, fwid), F32), jax.ShapeDtypeStruct((bl, mixw), BF16)),
                (s5_spec, tile_spec(fwid), tile_spec(mixw)),
                seq, tm, scratch=(s5_scratch,))
            conv5 = _s5_scan(u5, _s5_weights(s5_lambda_re[i], s5_lambda_im[i], s5_log_step[i], s5_b_re[i],
                                             s5_b_im[i], s5_c_re[i], s5_c_im[i]))
            zpad = jnp.zeros((LANES - DECAY_LORA, rw), F32)
            ones_bd = jnp.kron(jnp.eye(rw // RWKV_HEAD, dtype=F32), jnp.ones((RWKV_HEAD, RWKV_HEAD), F32))
            prm = dict(mu=row1(rwkv_mu[i]), w0=row1(rwkv_w0[i]),
                       wup=jnp.concatenate([rwkv_w_up[i], zpad], axis=0).astype(BF16),
                       a0=row1(rwkv_a0[i]),
                       aup=jnp.concatenate([zpad, rwkv_a_up[i]], axis=0).astype(BF16),
                       gup=rwkv_g_up[i].astype(BF16), kk=row1(rwkv_k_k[i]), ka=row1(rwkv_k_a[i]),
                       rk=row1(rwkv_r_k[i]), lnw=row1(rwkv_ln_w[i]), lnb=row1(rwkv_ln_b[i]),
                       ones=ones_bd.astype(BF16))
            yb = _rwkv(feats, prm, bsz, seq, lb, inv_passes, nbat)
            ins = (conv5, u5, yb, z, x2, gate, row1(s5_d[i]), s5_glu_w[i].astype(BF16), row1(s5_glu_b[i]), wo, fw)
            specs = [s5_spec, s5_spec,
                     tile_spec(rw), tile_spec(mixw), tile_spec(d), gate_spec] + [res_spec(a) for a in ins[6:]]
            kern = functools.partial(_outproj_even_kernel, final=final)
            out_scratch = [s5_scratch, s5_scratch]
        else:
            w_in = odd_w_in[i]
            q_end = 3 * mixw
            ba_w = jnp.concatenate([w_in[:, q_end:q_end + 2 * nh_gdn],
                                    jnp.zeros((d, LANES - 2 * nh_gdn), F32)], axis=1)
            w_r = jnp.concatenate([w_in[:, :q_end], w_in[:, q_end + 2 * nh_gdn:], ba_w], axis=1).astype(BF16)
            qkv, z, ba = _inproj(
                x2, nw, shift, scale, w_r,
                functools.partial(_inproj_odd_kernel, tn=512),
                (jax.ShapeDtypeStruct((bl, q_end), F32), jax.ShapeDtypeStruct((bl, mixw), BF16),
                 jax.ShapeDtypeStruct((bl, LANES), F32)),
                (tile_spec(q_end), tile_spec(mixw), tile_spec(LANES)),
                seq, tm)
            pad_row = lambda a: jnp.zeros((1, LANES), F32).at[0, nh_gdn:2 * nh_gdn].set(a)
            prm = dict(cw=gdn_conv_w[i], alog=pad_row(gdn_a_log[i]), dtb=pad_row(gdn_dt_bias[i]),
                       nw=row1(gdn_norm_w[i]))
            y = _gdn(qkv, ba, prm, bsz, seq, lb, inv_passes, nbat)
            ins = (y, z, x2, gate, wo, fw)
            specs = [tile_spec(mixw), tile_spec(mixw), tile_spec(d), gate_spec] + [res_spec(a) for a in ins[4:]]
            kern = functools.partial(_outproj_odd_kernel, final=final)
            out_scratch = []
        x2 = pl.pallas_call(
            kern,
            grid=(bl // tm,),
            in_specs=specs,
            out_specs=tile_spec(d),
            out_shape=jax.ShapeDtypeStruct((bl, d), F32),
            scratch_shapes=out_scratch,
            compiler_params=_cparams(("arbitrary",)),
            name="out_proj",
        )(*ins)
    return x2.reshape(bsz, seq, d)
```

```python
import functools
import math

import jax
import jax.numpy as jnp
from jax import lax
from jax.experimental import pallas as pl
from jax.experimental.pallas import tpu as pltpu

F32 = jnp.float32
BF16 = jnp.bfloat16

NORM_EPS = 1e-6
LANES = 128
SUBLANES = 8
VMEM_LIMIT = 56 * 1024 * 1024

S5_GROUP = 16
S5_STATE = 64
RWKV_HEAD = 64
DECAY_LORA = 64
ICLR_LORA = 64
GATE_LORA = 128
RWKV_DECAY_SCALE = math.exp(-0.5)
GDN_HEAD = 128
GDN_CONV = 4
CHUNK = 64
S5_T = 8

NN = (((1,), (0,)), ((), ()))
NT = (((1,), (1,)), ((), ()))
BNN = (((2,), (1,)), ((0,), (0,)))
BNT = (((2,), (2,)), ((0,), (0,)))


def _split(a, n):
    if a.dtype == BF16:
        return [a]
    parts = []
    r = a
    for i in range(n):
        p = r.astype(BF16)
        parts.append(p)
        if i + 1 < n:
            r = r - p.astype(F32)
    return parts


def _mm(a, b, dn=NN, pa=1, pb=1):
    ap = _split(a, pa)
    bp = _split(b, pb)
    lim = max(len(ap), len(bp))
    acc = None
    for i, x in enumerate(ap):
        for j, y in enumerate(bp):
            if i + j >= lim:
                continue
            t = lax.dot_general(x, y, dn, preferred_element_type=F32)
            acc = t if acc is None else acc + t
    return acc


def _sigmoid(x):
    return 0.5 + 0.5 * jnp.tanh(0.5 * x)


def _silu(x):
    h = 0.5 * x
    return h + h * jnp.tanh(h)


def _softplus(x):
    return jnp.maximum(x, 0.0) + jnp.log(1.0 + jnp.exp(-jnp.abs(x)))


def _gelu_tanh(x):
    c = math.sqrt(2.0 / math.pi)
    return 0.5 * x * (1.0 + jnp.tanh(c * (x + 0.044715 * (x * x * x))))


def _iota2(shape, dim):
    return lax.broadcasted_iota(jnp.int32, shape, dim)


def _inv_unit_lower(n_mat, eye, passes, dn, row, col):
    t = eye + jnp.where((row // 2) == (col // 2), n_mat, 0.0)
    b = 2
    while b < CHUNK:
        off = ((row // (2 * b)) == (col // (2 * b))) & ((row // b) % 2 == 1) & ((col // b) % 2 == 0)
        x = _mm(jnp.where(off, n_mat, 0.0), t, dn, pa=passes, pb=passes)
        t = t + _mm(t, x, dn, pa=passes, pb=passes)
        b *= 2
    return t


def _cparams(sem):
    return pltpu.CompilerParams(dimension_semantics=sem, vmem_limit_bytes=VMEM_LIMIT)


def _mod_kernel(c_ref, w_ref, b_ref, o_ref):
    s = _silu(c_ref[...])
    o_ref[0] = _mm(s, w_ref[0], pa=2, pb=2) + b_ref[0]


def _modulation(c, ada_w, ada_b):
    depth, d, d3 = ada_w.shape
    bsz = c.shape[0]
    nj = d3 // d
    return pl.pallas_call(
        _mod_kernel,
        grid=(depth, nj),
        in_specs=[
            pl.BlockSpec((bsz, d), lambda l, j: (0, 0)),
            pl.BlockSpec((1, d, d), lambda l, j: (l, 0, j)),
            pl.BlockSpec((1, 1, d), lambda l, j: (l, 0, j)),
        ],
        out_specs=pl.BlockSpec((1, bsz, d), lambda l, j: (l, 0, j)),
        out_shape=jax.ShapeDtypeStruct((depth, bsz, d3), F32),
        compiler_params=_cparams(("arbitrary", "arbitrary")),
        name="adaln_mod",
    )(c, ada_w, ada_b.reshape(depth, 1, d3))


def _norm_mod(x, nw, scale, shift):
    ms = jnp.mean(x * x, axis=-1, keepdims=True)
    h = x * lax.rsqrt(ms + NORM_EPS) * nw
    return h * (1.0 + scale) + shift


def _inproj_even_kernel(x_ref, nw_ref, sh_ref, sc_ref, w_ref, u_ref, f_ref, z_ref, us_ref, *, tn):
    hb = _norm_mod(x_ref[...], nw_ref[...], sc_ref[0], sh_ref[0]).astype(BF16)
    nu = u_ref.shape[0]
    nrow = u_ref.shape[2]
    for j in range(nu):
        us_ref[j] = _mm(hb, w_ref[:, j * LANES:(j + 1) * LANES])
        for t in range(S5_T):
            u_ref[j, 0, :, t * LANES:(t + 1) * LANES] = (
                us_ref[j, pl.ds(t, nrow, stride=S5_T), :].astype(u_ref.dtype))
    off = nu * LANES
    for o_ref in (f_ref, z_ref):
        width = o_ref.shape[1]
        for j0 in range(0, width, tn):
            w = min(tn, width - j0)
            o_ref[:, j0:j0 + w] = _mm(hb, w_ref[:, off + j0:off + j0 + w]).astype(o_ref.dtype)
        off += width


def _inproj_odd_kernel(x_ref, nw_ref, sh_ref, sc_ref, w_ref, qkv_ref, z_ref, ba_ref, *, tn):
    hb = _norm_mod(x_ref[...], nw_ref[...], sc_ref[0], sh_ref[0]).astype(BF16)
    off = 0
    for o_ref in (qkv_ref, z_ref, ba_ref):
        width = o_ref.shape[1]
        for j0 in range(0, width, tn):
            w = min(tn, width - j0)
            o_ref[:, j0:j0 + w] = _mm(hb, w_ref[:, off + j0:off + j0 + w]).astype(o_ref.dtype)
        off += width


def _inproj(x2, nw, shift, scale, w_bf, kernel_fn, out_shapes, out_specs, seq, tm, scratch=(), extra=()):
    bl, d = x2.shape
    n = w_bf.shape[1]
    per_b = seq // tm
    return pl.pallas_call(
        kernel_fn,
        grid=(bl // tm,),
        in_specs=[
            pl.BlockSpec((tm, d), lambda i: (i, 0)),
            pl.BlockSpec((1, d), lambda i: (0, 0)),
            pl.BlockSpec((1, 1, d), lambda i: (i // per_b, 0, 0)),
            pl.BlockSpec((1, 1, d), lambda i: (i // per_b, 0, 0)),
            pl.BlockSpec((d, n), lambda i: (0, 0)),
        ] + [pl.BlockSpec(a.shape, lambda i: (0, 0)) for a in extra],
        out_specs=out_specs,
        out_shape=out_shapes,
        scratch_shapes=list(scratch),
        compiler_params=_cparams(("arbitrary",)),
        name="in_proj",
    )(x2, nw, shift, scale, w_bf, *extra)


def _s5_kernel(u_ref, kd_ref, p_ref, q_ref, are_ref, aim_ref, y_ref, st_ref, sloc_ref, sprev_ref, m_ref, *, tc, nb):
    cb = pl.program_id(1)

    @pl.when(cb == 0)
    def _():
        st_ref[...] = jnp.zeros_like(st_ref)
        for s in range(S5_T):
            for t in range(S5_T):
                blk = kd_ref[t - s, 0].astype(BF16) if t >= s else jnp.zeros((LANES, LANES), BF16)
                m_ref[s * LANES:(s + 1) * LANES, t * LANES:(t + 1) * LANES] = blk

    kcols = u_ref.shape[-1]
    x = u_ref[0].reshape(nb * tc, kcols).astype(BF16)
    y = _mm(x, m_ref[...])
    sloc = _mm(x, p_ref[0])
    nt2 = sloc_ref.shape[0]
    nt = nt2 // 2
    pitch = tc + SUBLANES
    for k in range(nt2):
        for b in range(nb):
            sloc_ref[k, b * pitch:b * pitch + tc, :] = sloc[b * tc:(b + 1) * tc, k * LANES:(k + 1) * LANES]
    a_re = are_ref[0]
    a_im = aim_ref[0]

    def body(c, carry):
        rows = pl.ds(c, nb, stride=pitch)
        new = []
        for k in range(nt):
            s_re, s_im = carry[k], carry[nt + k]
            sprev_ref[k, rows, :] = s_re
            sprev_ref[nt + k, rows, :] = s_im
            ar = a_re[:, k * LANES:(k + 1) * LANES]
            ai = a_im[:, k * LANES:(k + 1) * LANES]
            new.append((ar * s_re - ai * s_im + sloc_ref[k, rows, :],
                        ar * s_im + ai * s_re + sloc_ref[nt + k, rows, :]))
        return tuple(n[0] for n in new) + tuple(n[1] for n in new)

    st = lax.fori_loop(0, tc, body, tuple(st_ref[k] for k in range(nt2)))
    for k in range(nt2):
        st_ref[k] = st[k]
    sprev = jnp.concatenate(
        [jnp.concatenate([sprev_ref[k, b * pitch:b * pitch + tc, :] for b in range(nb)], axis=0)
         for k in range(nt2)], axis=1)
    y = y + _mm(sprev, q_ref[0])
    y_ref[0] = y.reshape(nb, tc, kcols).astype(y_ref.dtype)


def _s5_weights(lambda_re, lambda_im, log_step, b_re, b_im, c_re, c_im):
    g_all, p_all = lambda_re.shape
    t_len = S5_T
    gpt = LANES // S5_GROUP
    nj = g_all // gpt
    lam = lax.complex(jnp.minimum(lambda_re, -1e-4), lambda_im)
    step = jnp.exp(log_step)[:, None]
    lam_bar = jnp.exp(lam * step)
    b_bar = ((lam_bar - 1.0) / lam)[..., None] * lax.complex(b_re, b_im)
    c_mat = lax.complex(c_re, c_im)
    taus = jnp.arange(t_len + 1, dtype=F32)
    pw = jnp.exp((lam * step)[None] * taus[:, None, None])
    hi = lax.Precision.HIGHEST
    kern = jnp.real(jnp.einsum('gcp,tgp,gpd->tgcd', c_mat, pw[:t_len], b_bar, precision=hi))
    s_idx = jnp.arange(t_len)
    eye_g = jnp.eye(gpt, dtype=F32)
    kd = jnp.einsum('tjgoi,hg->tjhigo', kern.reshape(t_len, nj, gpt, S5_GROUP, S5_GROUP), eye_g)
    kd = kd.reshape(t_len, nj, LANES, LANES)
    pb = pw[t_len - 1 - s_idx][:, :, :, None] * b_bar[None]
    pb = jnp.stack([jnp.real(pb), jnp.imag(pb)], axis=0)
    pb = pb.reshape(2, t_len, nj, gpt, p_all, S5_GROUP)
    p_mat = jnp.einsum('rsjgpi,hg->jshirgp', pb, eye_g).astype(BF16).reshape(nj, t_len * LANES, 2 * gpt * p_all)
    cq = c_mat[None] * pw[1:t_len + 1][:, :, None, :]
    cq = jnp.stack([jnp.real(cq), -jnp.imag(cq)], axis=0)
    cq = cq.reshape(2, t_len, nj, gpt, S5_GROUP, p_all)
    q_mat = jnp.einsum('rtjgop,hg->jrhptgo', cq, eye_g).astype(BF16).reshape(nj, 2 * gpt * p_all, t_len * LANES)
    a_t = pw[t_len].reshape(nj, 1, gpt * p_all)
    return (kd, p_mat, q_mat,
            jnp.real(a_t).astype(F32), jnp.imag(a_t).astype(F32))


def _s5_scan(u5, weights):
    kd, p_mat, q_mat, a_re, a_im = weights
    nj, bsz, nchunks, kcols = u5.shape
    tc = min(64, nchunks)
    ns2 = p_mat.shape[-1]
    kern = functools.partial(_s5_kernel, tc=tc, nb=bsz)
    return pl.pallas_call(
        kern,
        grid=(nj, nchunks // tc),
        in_specs=[
            pl.BlockSpec((1, bsz, tc, kcols), lambda j, c: (j, 0, c, 0)),
            pl.BlockSpec((S5_T, 1, LANES, LANES), lambda j, c: (0, j, 0, 0)),
            pl.BlockSpec((1, kcols, ns2), lambda j, c: (j, 0, 0)),
            pl.BlockSpec((1, ns2, kcols), lambda j, c: (j, 0, 0)),
            pl.BlockSpec((1, 1, ns2 // 2), lambda j, c: (j, 0, 0)),
            pl.BlockSpec((1, 1, ns2 // 2), lambda j, c: (j, 0, 0)),
        ],
        out_specs=pl.BlockSpec((1, bsz, tc, kcols), lambda j, c: (j, 0, c, 0)),
        out_shape=jax.ShapeDtypeStruct((nj, bsz, nchunks, kcols), BF16),
        scratch_shapes=[
            pltpu.VMEM((ns2 // LANES, bsz, LANES), F32),
            pltpu.VMEM((ns2 // LANES, bsz * (tc + SUBLANES), LANES), F32),
            pltpu.VMEM((ns2 // LANES, bsz * (tc + SUBLANES), LANES), F32),
            pltpu.VMEM((kcols, kcols), BF16),
        ],
        compiler_params=_cparams(("arbitrary", "arbitrary")),
        name="s5_scan",
    )(u5, kd, p_mat, q_mat, a_re, a_im)


def _rwkv_kernel(f_ref, mu_ref, w0_ref, wup_ref, a0_ref, aup_ref, gup_ref, kk_ref, ka_ref, rk_ref,
                 lnw_ref, lnb_ref, ones_ref, o_ref,
                 fbuf, state, a_st, r_st, b_st, k_st, v_st, arw_st, tav_st, rkv_st, arb_st, bkd_st, vk_st,
                 ee_st, o_s, *, lb, inv_passes):
    t = pl.program_id(1)
    nbat = f_ref.shape[0]
    width = o_s.shape[1]
    npair = width // LANES
    nchain = nbat * npair
    nc = lb // CHUNK
    c2 = 2 * CHUNK

    @pl.when(t == 0)
    def _():
        fbuf[:, 0:SUBLANES, :] = jnp.zeros((nbat, SUBLANES, fbuf.shape[2]), F32)
        state[...] = jnp.zeros_like(state)

    @pl.when(t > 0)
    def _():
        fbuf[:, 0:SUBLANES, :] = fbuf[:, lb:lb + SUBLANES, :]

    xl, xsl = [], []
    for bi in range(nbat):
        xb = f_ref[bi]
        fbuf[bi, SUBLANES:SUBLANES + lb, :] = xb
        xl.append(xb)
        xsl.append(fbuf[bi, pl.ds(SUBLANES - 1, lb), :])
    x = jnp.concatenate(xl, axis=0)
    xs = jnp.concatenate(xsl, axis=0)
    f = x + mu_ref[...] * (xs - x)
    r = f[:, 0:width]
    k = f[:, width:2 * width]
    v = f[:, 2 * width:3 * width]
    lora = f[:, 3 * width:3 * width + LANES]
    xg = f[:, 3 * width + LANES:3 * width + 2 * LANES]
    lw = -RWKV_DECAY_SCALE * _sigmoid(w0_ref[...] + _mm(jnp.tanh(lora), wup_ref[...]))
    a = _sigmoid(a0_ref[...] + _mm(lora, aup_ref[...]))
    g = _mm(_sigmoid(xg), gup_ref[...])
    half = width // 2
    ones_half = ones_ref[0:half, 0:half]

    def head_sum(m):
        return jnp.concatenate([_mm(m[:, 0:half], ones_half), _mm(m[:, half:width], ones_half)], axis=1)

    kk = k * kk_ref[...]
    kn = kk * lax.rsqrt(head_sum(kk * kk) + 1e-6)
    kp = k * (1.0 + (a - 1.0) * ka_ref[...])
    bonus = head_sum(r * kp * rk_ref[...]) * v
    bb = kn * a

    row = _iota2((c2, c2), 0)
    col = _iota2((c2, c2), 1)
    same = (row // CHUNK) == (col // CHUNK)
    strict = same & (row > col)
    incl = same & (row >= col)
    eye = jnp.where(row == col, 1.0, 0.0).astype(F32)
    tri = jnp.where(_iota2((CHUNK, CHUNK), 0) >= _iota2((CHUNK, CHUNK), 1), 1.0, 0.0).astype(BF16)
    first = _iota2((CHUNK, LANES), 1) < RWKV_HEAD

    def stack(m):
        return jnp.concatenate([jnp.where(first, m, 0.0), jnp.where(first, 0.0, m)], axis=0)

    for c in range(nc):
        for bi in range(nbat):
            cs = slice(bi * lb + c * CHUNK, bi * lb + (c + 1) * CHUNK)
            lwc = lw[cs]
            gc = _mm(tri, lwc, pb=3)
            e_g = jnp.exp(gc)
            e_ng = jnp.exp(-gc)
            at = -kn[cs] * jnp.exp(gc - lwc)
            bt = bb[cs] * e_ng
            kt = kp[cs] * e_ng
            rt = r[cs] * e_g
            vv = v[cs]
            e_end = jnp.broadcast_to(jnp.exp(gc[CHUNK - 1:CHUNK, :]), (SUBLANES, width))
            for p in range(npair):
                idx = c * nchain + bi * npair + p
                sl = slice(p * LANES, (p + 1) * LANES)
                a_st[idx] = stack(at[:, sl])
                r_st[idx] = stack(rt[:, sl])
                b_st[idx] = stack(bt[:, sl])
                k_st[idx] = stack(kt[:, sl])
                v_st[idx] = stack(vv[:, sl])
                ee_st[idx] = e_end[:, sl]

    a_all = a_st[...]
    r_all = r_st[...]
    v_all = v_st[...]
    ar = jnp.concatenate([a_all, r_all], axis=1)
    bk = jnp.concatenate([b_st[...], k_st[...]], axis=1)
    p4 = _mm(ar, bk, BNT)
    a_ab = jnp.where(strict, p4[:, 0:c2, 0:c2], 0.0)
    a_ak = jnp.where(strict, p4[:, 0:c2, c2:2 * c2], 0.0)
    a_rb = jnp.where(incl, p4[:, c2:2 * c2, 0:c2], 0.0)
    a_rk = jnp.where(incl, p4[:, c2:2 * c2, c2:2 * c2], 0.0)
    t_inv = _inv_unit_lower(a_ab, eye, inv_passes, BNN, row, col)
    tt = _mm(t_inv, jnp.concatenate([a_all, _mm(a_ak, v_all, BNN)], axis=2), BNN, pa=inv_passes)
    arw_st[...] = jnp.concatenate([tt[:, :, 0:LANES], r_all], axis=1)
    tav_st[...] = tt[:, :, LANES:2 * LANES]
    rkv_st[...] = _mm(a_rk, v_all, BNN)
    arb_st[...] = a_rb
    bkd = bk * ee_st[:, 0:1, :]
    bkd_st[...] = bkd[:, 0:c2, :]
    vk_st[...] = _mm(jnp.swapaxes(v_all, 1, 2), bkd[:, c2:2 * c2, :], BNN)

    def chunk_body(c, carry):
        xr = [_mm(arw_st[c * nchain + q], state[q], NT) for q in range(nchain)]
        uu = [xr[q][0:c2] + tav_st[c * nchain + q] for q in range(nchain)]
        ob = [_mm(arb_st[c * nchain + q], uu[q]) for q in range(nchain)]
        for q in range(nchain):
            bi, p = divmod(q, npair)
            idx = c * nchain + q
            rows = pl.ds(pl.multiple_of(bi * lb + c * CHUNK, CHUNK), CHUNK)
            o_st = xr[q][c2:2 * c2] + ob[q] + rkv_st[idx]
            o_s[rows, p * LANES:(p + 1) * LANES] = o_st[0:CHUNK] + o_st[CHUNK:c2]
            state[q] = state[q] * ee_st[idx, 0:1, :] + vk_st[idx] + _mm(uu[q].T, bkd_st[idx])
        return carry

    lax.fori_loop(0, nc, chunk_body, 0)

    o = o_s[...]
    inv_n = 1.0 / RWKV_HEAD
    mean = head_sum(o) * inv_n
    d = o - mean
    var = head_sum(d * d) * inv_n
    on = d * lax.rsqrt(var + 1e-5 * RWKV_HEAD) * lnw_ref[...] + lnb_ref[...]
    out = ((on + bonus) * g).astype(o_ref.dtype)
    for bi in range(nbat):
        o_ref[bi] = out[bi * lb:(bi + 1) * lb]


def _rwkv(feats, prm, bsz, seq, lb, inv_passes, nbat):
    bl, fw = feats.shape
    width = prm['w0'].shape[1]
    full = lambda a: pl.BlockSpec(a.shape, lambda i, t: (0, 0))
    names = ['mu', 'w0', 'wup', 'a0', 'aup', 'gup', 'kk', 'ka', 'rk', 'lnw', 'lnb', 'ones']
    per_b = seq // lb
    nchain = nbat * (width // LANES)
    n = (lb // CHUNK) * nchain
    c2 = 2 * CHUNK
    kern = functools.partial(_rwkv_kernel, lb=lb, inv_passes=inv_passes)
    vm = lambda *s: pltpu.VMEM(s, F32)
    out = pl.pallas_call(
        kern,
        grid=(bsz // nbat, per_b),
        in_specs=[pl.BlockSpec((nbat, lb, fw), lambda i, t: (i, t, 0))] + [full(prm[k]) for k in names],
        out_specs=pl.BlockSpec((nbat, lb, width), lambda i, t: (i, t, 0)),
        out_shape=jax.ShapeDtypeStruct((bsz, seq, width), BF16),
        scratch_shapes=[vm(nbat, lb + SUBLANES, fw), vm(nchain, LANES, LANES)]
                       + [vm(n, c2, LANES)] * 5
                       + [vm(n, 2 * c2, LANES), vm(n, c2, LANES), vm(n, c2, LANES), vm(n, c2, c2),
                          vm(n, c2, LANES), vm(n, LANES, LANES), vm(n, SUBLANES, LANES), vm(nbat * lb, width)],
        compiler_params=_cparams(("arbitrary", "arbitrary")),
        name="rwkv7",
    )(feats.reshape(bsz, seq, fw), *[prm[k] for k in names])
    return out.reshape(bl, width)


def _gdn_kernel(qkv_ref, ba_ref, cw_ref, alog_ref, dtb_ref, nw_ref, o_ref,
                cbuf, state, q_st, k_st, v_st, u_st, w_st, qd_st, ket_st, in_st, cd_st, o_s, *, lb, nh, inv_passes):
    t = pl.program_id(1)
    nbat = qkv_ref.shape[0]
    hd = GDN_HEAD
    nc = lb // CHUNK
    npair = nh // 2
    npt = nbat * npair
    c2 = 2 * CHUNK
    ntile = 3 * nh
    nrow = lb // SUBLANES
    per = CHUNK // SUBLANES

    @pl.when(t == 0)
    def _():
        cbuf[:, 0:SUBLANES, :] = jnp.zeros((nbat * ntile, SUBLANES, LANES), F32)
        state[...] = jnp.zeros_like(state)

    @pl.when(t > 0)
    def _():
        cbuf[:, 0:SUBLANES, :] = cbuf[:, lb:lb + SUBLANES, :]

    for bi in range(nbat):
        for lt in range(ntile):
            ct = bi * ntile + lt
            cbuf[ct, SUBLANES:SUBLANES + lb, :] = qkv_ref[bi, :, lt * LANES:(lt + 1) * LANES]
            slab = {s: cbuf[ct, pl.ds(SUBLANES + s, nrow, stride=SUBLANES), :]
                    for s in range(1 - GDN_CONV, SUBLANES)}
            cwt = [cw_ref[j:j + 1, lt * LANES:(lt + 1) * LANES] for j in range(GDN_CONV)]
            kind, h = divmod(lt, nh)
            dst = (q_st, k_st, v_st)[kind]
            hp, i = divmod(h, 2)
            for s in range(SUBLANES):
                acc = cwt[0] * slab[s + 1 - GDN_CONV]
                for j in range(1, GDN_CONV):
                    acc = acc + cwt[j] * slab[s + 1 - GDN_CONV + j]
                y = _silu(acc)
                if kind == 0:
                    y = y * (lax.rsqrt(jnp.sum(y * y, axis=-1, keepdims=True) + 1e-6) * (hd ** -0.5))
                elif kind == 1:
                    y = y * lax.rsqrt(jnp.sum(y * y, axis=-1, keepdims=True) + 1e-6)
                for c in range(nc):
                    r0 = i * CHUNK + s * per
                    dst[c * npt + bi * npair + hp, r0:r0 + per, :] = y[c * per:(c + 1) * per]

    def time_of(idx):
        within = idx % CHUNK
        return (idx // CHUNK) * CHUNK + (within % per) * SUBLANES + within // per

    row = time_of(_iota2((c2, c2), 0))
    col = time_of(_iota2((c2, c2), 1))
    same = (row // CHUNK) == (col // CHUNK)
    strict = same & (row > col)
    incl = same & (row >= col)
    eye = jnp.where(row == col, 1.0, 0.0).astype(F32)
    tri = jnp.where(time_of(_iota2((CHUNK, CHUNK), 0)) >= time_of(_iota2((CHUNK, CHUNK), 1)), 1.0, 0.0).astype(BF16)
    lane_first = _iota2((1, c2), 1) < CHUNK
    row_first = _iota2((c2, hd), 0) < CHUNK

    gcols, grows, bcols, gends = [], [], [], []
    for c in range(nc):
        for bi in range(nbat):
            ba = jnp.concatenate([ba_ref[bi, pl.ds(c * CHUNK + s, per, stride=SUBLANES), :]
                                  for s in range(SUBLANES)], axis=0)
            bc = _sigmoid(ba)
            ld = -jnp.exp(alog_ref[...]) * _softplus(ba + dtb_ref[...])
            g = _mm(tri, ld, pb=3)
            g2t = jnp.concatenate([g, g], axis=0).T
            for hp in range(npair):
                l0 = nh + 2 * hp
                l1 = l0 + 1
                gcols.append(jnp.concatenate([g[:, l0:l0 + 1], g[:, l1:l1 + 1]], axis=0))
                grows.append(jnp.where(lane_first, g2t[l0:l0 + 1, :], g2t[l1:l1 + 1, :]))
                bcols.append(jnp.concatenate([bc[:, 2 * hp:2 * hp + 1], bc[:, 2 * hp + 1:2 * hp + 2]], axis=0))
                gends.append(jnp.concatenate([jnp.broadcast_to(g[CHUNK - 1:CHUNK, l0:l0 + 1], (CHUNK, 1)),
                                              jnp.broadcast_to(g[CHUNK - 1:CHUNK, l1:l1 + 1], (CHUNK, 1))], axis=0))
    gcol = jnp.stack(gcols)
    grow = jnp.stack(grows)
    bcol = jnp.stack(bcols)
    gend = jnp.stack(gends)

    q_all = q_st[...]
    k_all = k_st[...]
    v_all = v_st[...]
    dec = jnp.where(incl, jnp.exp(jnp.where(incl, gcol - grow, 0.0)), 0.0)
    e_g = jnp.exp(gcol)
    kb = k_all * bcol
    a_mat = jnp.where(strict, _mm(kb, k_all, BNT) * dec, 0.0)
    in_st[...] = jnp.where(incl, _mm(q_all, k_all, BNT) * dec, 0.0)
    t_inv = _inv_unit_lower(-a_mat, eye, inv_passes, BNN, row, col)
    uw = _mm(t_inv, jnp.concatenate([v_all * bcol, kb * e_g], axis=2), BNN, pa=inv_passes)
    u_st[...] = uw[:, :, 0:hd]
    w_st[...] = uw[:, :, hd:2 * hd]
    qd_st[...] = q_all * e_g
    ket_st[...] = jnp.swapaxes(k_all * jnp.exp(gend - gcol), 1, 2)
    cd_st[...] = jnp.broadcast_to(jnp.exp(gend), cd_st.shape)

    def chunk_body(c, carry):
        wq = []
        for hh in range(nbat * nh):
            pp, i = divmod(hh, 2)
            rs = slice(i * CHUNK, (i + 1) * CHUNK)
            lhs = jnp.concatenate([w_st[c * npt + pp, rs, :], qd_st[c * npt + pp, rs, :]], axis=0)
            wq.append(_mm(lhs, state[hh]))
        vn = []
        for pp in range(npt):
            w0 = wq[2 * pp]
            w1 = wq[2 * pp + 1]
            vn.append(u_st[c * npt + pp] - jnp.concatenate([w0[0:CHUNK], w1[0:CHUNK]], axis=0))
        oi = [_mm(in_st[c * npt + pp], vn[pp]) for pp in range(npt)]
        for pp in range(npt):
            idx = c * npt + pp
            bi, hp = divmod(pp, npair)
            kv = _mm(ket_st[idx], jnp.concatenate([jnp.where(row_first, vn[pp], 0.0),
                                                   jnp.where(row_first, 0.0, vn[pp])], axis=1))
            for i in range(2):
                hh = 2 * pp + i
                h = 2 * hp + i
                rs = slice(i * CHUNK, (i + 1) * CHUNK)
                state[hh] = state[hh] * cd_st[idx, i * CHUNK:i * CHUNK + 1, :] + kv[:, i * hd:(i + 1) * hd]
                o_h = wq[hh][CHUNK:c2] + oi[pp][rs]
                o_h = o_h * lax.rsqrt(jnp.mean(o_h * o_h, axis=-1, keepdims=True) + NORM_EPS) * nw_ref[...]
                for s in range(SUBLANES):
                    o_s[bi * nh + h, pl.ds(c * CHUNK + s, per, stride=SUBLANES), :] = o_h[s * per:(s + 1) * per]
        return carry

    lax.fori_loop(0, nc, chunk_body, 0)
    for bi in range(nbat):
        for h in range(nh):
            o_ref[bi, :, h * hd:(h + 1) * hd] = o_s[bi * nh + h].astype(o_ref.dtype)


def _gdn(qkv, ba, prm, bsz, seq, lb, inv_passes, nbat):
    bl, w3 = qkv.shape
    mixw = w3 // 3
    nh = mixw // GDN_HEAD
    per_b = seq // lb
    n = (lb // CHUNK) * nbat * (nh // 2)
    c2 = 2 * CHUNK
    full = lambda a: pl.BlockSpec(a.shape, lambda i, t: (0, 0))
    kern = functools.partial(_gdn_kernel, lb=lb, nh=nh, inv_passes=inv_passes)
    vm = lambda *s: pltpu.VMEM(s, F32)
    names = ['cw', 'alog', 'dtb', 'nw']
    out = pl.pallas_call(
        kern,
        grid=(bsz // nbat, per_b),
        in_specs=[pl.BlockSpec((nbat, lb, w3), lambda i, t: (i, t, 0)),
                  pl.BlockSpec((nbat, lb, LANES), lambda i, t: (i, t, 0))] + [full(prm[k]) for k in names],
        out_specs=pl.BlockSpec((nbat, lb, mixw), lambda i, t: (i, t, 0)),
        out_shape=jax.ShapeDtypeStruct((bsz, seq, mixw), BF16),
        scratch_shapes=[vm(nbat * 3 * nh, lb + SUBLANES, LANES), vm(nbat * nh, GDN_HEAD, GDN_HEAD)]
                       + [vm(n, c2, GDN_HEAD)] * 6
                       + [vm(n, GDN_HEAD, c2), vm(n, c2, c2), vm(n, c2, GDN_HEAD), vm(nbat * nh, lb, GDN_HEAD)],
        compiler_params=_cparams(("arbitrary", "arbitrary")),
        name="gdn",
    )(qkv.reshape(bsz, seq, w3), ba.reshape(bsz, seq, LANES), *[prm[k] for k in names])
    return out.reshape(bl, mixw)


def _finish(x_ref, g_ref, acc, fw_ref, o_ref, final):
    xn = x_ref[...] + g_ref[0] * acc
    if final:
        ms = jnp.mean(xn * xn, axis=-1, keepdims=True)
        xn = xn * lax.rsqrt(ms + NORM_EPS) * fw_ref[...]
    o_ref[...] = xn


def _outproj_even_kernel(conv_ref, u_ref, yb_ref, z_ref, x_ref, g_ref, d_ref, gw_ref, gb_ref, w_ref, fw_ref,
                         o_ref, cs_ref, us_ref, *, final):
    nj = conv_ref.shape[0]
    nrow = conv_ref.shape[2]
    for j in range(nj):
        for t in range(S5_T):
            cs_ref[j, pl.ds(t, nrow, stride=S5_T), :] = conv_ref[j, 0, :, t * LANES:(t + 1) * LANES].astype(F32)
            us_ref[j, pl.ds(t, nrow, stride=S5_T), :] = u_ref[j, 0, :, t * LANES:(t + 1) * LANES].astype(F32)
    ya = jnp.concatenate([cs_ref[j] for j in range(nj)], axis=1)
    uu = jnp.concatenate([us_ref[j] for j in range(nj)], axis=1)
    ya = _gelu_tanh(ya + d_ref[...] * uu)
    ya = ya * _sigmoid(_mm(ya, gw_ref[...]) + gb_ref[...])
    wa = ya.shape[1]
    sz = _silu(z_ref[...].astype(F32))
    acc = _mm(ya * sz[:, :wa], w_ref[0:wa, :]) + _mm(yb_ref[...].astype(F32) * sz[:, wa:], w_ref[wa:, :])
    _finish(x_ref, g_ref, acc, fw_ref, o_ref, final)


def _outproj_odd_kernel(y_ref, z_ref, x_ref, g_ref, w_ref, fw_ref, o_ref, *, final):
    acc = _mm(y_ref[...].astype(F32) * _silu(z_ref[...].astype(F32)), w_ref[...])
    _finish(x_ref, g_ref, acc, fw_ref, o_ref, final)


def kernel(x, c, norm_w, ada_w, ada_b, w_out, final_norm_w, even_w_in, s5_lambda_re, s5_lambda_im, s5_log_step, s5_b_re, s5_b_im, s5_c_re, s5_c_im, s5_d, s5_glu_w, s5_glu_b, rwkv_mu, rwkv_w0, rwkv_w_up, rwkv_a0, rwkv_a_up, rwkv_g_up, rwkv_k_k, rwkv_k_a, rwkv_r_k, rwkv_ln_w, rwkv_ln_b, odd_w_in, gdn_conv_w, gdn_a_log, gdn_dt_bias, gdn_norm_w):
    bsz, seq, d = x.shape
    depth = norm_w.shape[0]
    bl = bsz * seq
    tm = min(512, seq)
    lb = min(256, seq)
    inv_passes = 1
    nbat = 2 if bsz % 2 == 0 else 1
    rwkv_nbat = 4 if bsz % 4 == 0 else nbat
    rwkv_lb = min(lb * nbat // rwkv_nbat, seq)
    s5w = s5_d.shape[1]
    nj = s5w // LANES
    rw = rwkv_w0.shape[1]
    mixw = w_out.shape[1]
    nh_gdn = mixw // GDN_HEAD
    per_b = seq // tm

    mod = _modulation(c, ada_w, ada_b)
    x2 = x.reshape(bl, d)
    row1 = lambda a: a.reshape(1, -1)
    tile_spec = lambda w: pl.BlockSpec((tm, w), lambda i: (i, 0))
    res_spec = lambda a: pl.BlockSpec(a.shape, lambda i: tuple(0 for _ in a.shape))
    gate_spec = pl.BlockSpec((1, 1, d), lambda i: (i // per_b, 0, 0))
    fw = row1(final_norm_w)

    for layer in range(depth):
        shift = mod[layer, :, 0:d].reshape(bsz, 1, d)
        scale = mod[layer, :, d:2 * d].reshape(bsz, 1, d)
        gate = mod[layer, :, 2 * d:3 * d].reshape(bsz, 1, d)
        i = layer // 2
        final = layer == depth - 1
        nw = row1(norm_w[layer])
        wo = w_out[layer].astype(BF16)
        if layer % 2 == 0:
            fwid = even_w_in.shape[2] - s5w - mixw
            s5_spec = pl.BlockSpec((nj, 1, tm // S5_T, S5_T * LANES), lambda r: (0, r // per_b, r % per_b, 0))
            s5_scratch = pltpu.VMEM((nj, tm, LANES), F32)
            u5, feats, z = _inproj(
                x2, nw, shift, scale, even_w_in[i].astype(BF16),
                functools.partial(_inproj_even_kernel, tn=512),
                (jax.ShapeDtypeStruct((nj, bsz, seq // S5_T, S5_T * LANES), BF16),
                 jax.ShapeDtypeStruct((bl, fwid), F32), jax.ShapeDtypeStruct((bl, mixw), BF16)),
                (s5_spec, tile_spec(fwid), tile_spec(mixw)),
                seq, tm, scratch=(s5_scratch,))
            conv5 = _s5_scan(u5, _s5_weights(s5_lambda_re[i], s5_lambda_im[i], s5_log_step[i], s5_b_re[i],
                                             s5_b_im[i], s5_c_re[i], s5_c_im[i]))
            zpad = jnp.zeros((LANES - DECAY_LORA, rw), F32)
            ones_bd = jnp.kron(jnp.eye(rw // RWKV_HEAD, dtype=F32), jnp.ones((RWKV_HEAD, RWKV_HEAD), F32))
            prm = dict(mu=row1(rwkv_mu[i]), w0=row1(rwkv_w0[i]),
                       wup=jnp.concatenate([rwkv_w_up[i], zpad], axis=0).astype(BF16),
                       a0=row1(rwkv_a0[i]),
                       aup=jnp.concatenate([zpad, rwkv_a_up[i]], axis=0).astype(BF16),
                       gup=rwkv_g_up[i].astype(BF16), kk=row1(rwkv_k_k[i]), ka=row1(rwkv_k_a[i]),
                       rk=row1(rwkv_r_k[i]), lnw=row1(rwkv_ln_w[i]), lnb=row1(rwkv_ln_b[i]),
                       ones=ones_bd.astype(BF16))
            yb = _rwkv(feats, prm, bsz, seq, rwkv_lb, inv_passes, rwkv_nbat)
            ins = (conv5, u5, yb, z, x2, gate, row1(s5_d[i]), s5_glu_w[i].astype(BF16), row1(s5_glu_b[i]), wo, fw)
            specs = [s5_spec, s5_spec,
                     tile_spec(rw), tile_spec(mixw), tile_spec(d), gate_spec] + [res_spec(a) for a in ins[6:]]
            kern = functools.partial(_outproj_even_kernel, final=final)
            out_scratch = [s5_scratch, s5_scratch]
        else:
            w_in = odd_w_in[i]
            q_end = 3 * mixw
            ba_w = jnp.concatenate([w_in[:, q_end:q_end + 2 * nh_gdn],
                                    jnp.zeros((d, LANES - 2 * nh_gdn), F32)], axis=1)
            w_r = jnp.concatenate([w_in[:, :q_end], w_in[:, q_end + 2 * nh_gdn:], ba_w], axis=1).astype(BF16)
            qkv, z, ba = _inproj(
                x2, nw, shift, scale, w_r,
                functools.partial(_inproj_odd_kernel, tn=512),
                (jax.ShapeDtypeStruct((bl, q_end), F32), jax.ShapeDtypeStruct((bl, mixw), BF16),
                 jax.ShapeDtypeStruct((bl, LANES), F32)),
                (tile_spec(q_end), tile_spec(mixw), tile_spec(LANES)),
                seq, tm)
            pad_row = lambda a: jnp.zeros((1, LANES), F32).at[0, nh_gdn:2 * nh_gdn].set(a)
            prm = dict(cw=gdn_conv_w[i], alog=pad_row(gdn_a_log[i]), dtb=pad_row(gdn_dt_bias[i]),
                       nw=row1(gdn_norm_w[i]))
            y = _gdn(qkv, ba, prm, bsz, seq, lb, inv_passes, nbat)
            ins = (y, z, x2, gate, wo, fw)
            specs = [tile_spec(mixw), tile_spec(mixw), tile_spec(d), gate_spec] + [res_spec(a) for a in ins[4:]]
            kern = functools.partial(_outproj_odd_kernel, final=final)
            out_scratch = []
        x2 = pl.pallas_call(
            kern,
            grid=(bl // tm,),
            in_specs=specs,
            out_specs=tile_spec(d),
            out_shape=jax.ShapeDtypeStruct((bl, d), F32),
            scratch_shapes=out_scratch,
            compiler_params=_cparams(("arbitrary",)),
            name="out_proj",
        )(*ins)
    return x2.reshape(bsz, seq, d)
```
